```python
import jax, jax.numpy as jnp
from jax import lax
import numpy as np

D_MODEL = 1024
BATCH = 2
SEQ = 8192
DEPTH = 2
DEC_BATCH = 128
DEC_SEQ = 4
PAST_LEN = 8192
PAGE_SIZE = 128

N_META = 16
HD = 64
H_A = 8
Q_RANK = 256
KV_RANK = 256
ROPE_DIM = 32
H_F = 8
W_C = 512
CONV_W = 3
H_R = 8
DK_R = 64
DV_R = 128
QB = 128
ROPE_BASE = 10000.0
EPS = 1e-6
NEG = -1e30

W_A = H_A * HD
W_B = H_F * HD
W_D = H_R * DV_R
BRANCH_WIDTHS = (W_A, W_B, W_C, W_D)
IN_SPLITS = (Q_RANK, KV_RANK, ROPE_DIM,
             H_F * HD, H_F * HD, H_F * HD, H_F,
             W_C, W_C, W_C,
             H_R * DK_R, H_R * DK_R, H_R * DV_R,
             W_A, W_B, W_C, W_D,
             D_MODEL, D_MODEL, D_MODEL, D_MODEL)
N_IN = sum(IN_SPLITS)

kernel_name = 'hybrid_mla_fox_conv_retention_step'


def _rmsnorm(x, g):
    xf = x.astype(jnp.float32)
    y = xf * lax.rsqrt(jnp.mean(xf * xf, axis=-1, keepdims=True) + EPS)
    return (y * g.astype(jnp.float32)).astype(x.dtype)


def _rotary(x, pos):
    half = x.shape[-1] // 2
    inv = ROPE_BASE ** (-jnp.arange(half, dtype=jnp.float32) / half)
    ang = pos.astype(jnp.float32)[:, None] * inv[None, :]
    cos = jnp.cos(ang)[:, None, :]
    sin = jnp.sin(ang)[:, None, :]
    x1 = x[..., :half].astype(jnp.float32)
    x2 = x[..., half:].astype(jnp.float32)
    return jnp.concatenate([x1 * cos - x2 * sin, x1 * sin + x2 * cos], axis=-1).astype(x.dtype)


def _ret_log_gamma():
    return jnp.log1p(-jnp.exp2(-5.0 - jnp.arange(H_R, dtype=jnp.float32)))


def _split_in(xn, w_in):
    p = jnp.einsum('btd,dn->btn', xn, w_in)
    return jnp.split(p, np.cumsum(IN_SPLITS)[:-1].tolist(), axis=-1)


def _gather_pages(pool, page_table):
    g = pool[page_table]
    return g.reshape((g.shape[0], g.shape[1] * g.shape[2]) + g.shape[3:])


def _mla_project(cq, ckv, kr, pos, lp):
    B, T, _ = cq.shape
    q = jnp.einsum('btr,rn->btn', _rmsnorm(cq, lp['g_cq']), lp['w_uq']).reshape(B, T, H_A, HD + ROPE_DIM)
    q = jnp.concatenate([_rmsnorm(q[..., :HD], lp['g_qn']),
                         _rotary(_rmsnorm(q[..., HD:], lp['g_qr']), pos)], axis=-1)
    ckv = _rmsnorm(ckv, lp['g_ckv'])
    kr = _rotary(_rmsnorm(kr, lp['g_kr'])[:, :, None, :], pos)[:, :, 0, :]
    return q, ckv, kr


def _mla_key_nope(ckv, lp):
    return _rmsnorm(jnp.einsum('bsr,rhd->bshd', ckv, lp['w_uk'].reshape(KV_RANK, H_A, HD)), lp['g_kn'])


def _fox_project(f_q, f_k, f_v, f_f, lp):
    B, T, _ = f_q.shape
    q = _rmsnorm(f_q.reshape(B, T, H_F, HD), lp['g_fq'])
    k = _rmsnorm(f_k.reshape(B, T, H_F, HD), lp['g_fk'])
    v = f_v.reshape(B, T, H_F, HD)
    logf = jax.nn.log_sigmoid(f_f.astype(jnp.float32) + lp['b_f'].astype(jnp.float32))
    return q, k, v, logf


def _ret_project(r_q, r_k, r_v, pos):
    B, T, _ = r_q.shape
    q = _rotary(r_q.reshape(B, T, H_R, DK_R), pos)
    k = _rotary(r_k.reshape(B, T, H_R, DK_R), pos) * (DK_R ** -0.5)
    v = r_v.reshape(B, T, H_R, DV_R)
    return q, k, v


def _short_conv(up, w):
    T = up.shape[1] - (CONV_W - 1)
    y = w[0] * up[:, 0:T]
    for j in range(1, CONV_W):
        y = y + w[j] * up[:, j:j + T]
    return y


def _retention_chunk(q, k, v, s):
    C = q.shape[1]
    lg = _ret_log_gamma()
    i = jnp.arange(C, dtype=jnp.float32)
    diff = i[:, None] - i[None, :]
    decay = jnp.where(diff >= 0, jnp.exp(jnp.maximum(diff, 0.0)[None] * lg[:, None, None]), 0.0).astype(q.dtype)
    qk = jnp.einsum('bihd,bjhd->bhij', q, k) * decay
    inner = jnp.einsum('bhij,bjhe->bihe', qk, v)
    q_dec = jnp.exp((i[:, None] + 1.0) * lg[None, :]).astype(q.dtype)
    cross = jnp.einsum('bihd,bhde->bihe', q, s) * q_dec[None, :, :, None]
    k_dec = jnp.exp((C - 1.0 - i)[:, None] * lg[None, :]).astype(q.dtype)
    s_new = (jnp.exp(C * lg).astype(q.dtype)[None, :, None, None] * s
             + jnp.einsum('bjhd,bjhe->bhde', k * k_dec[None, :, :, None], v))
    return s_new, inner + cross


def _sweep_attention(q, k, v, valid, fcum):
    B, T, H, dq = q.shape
    scale = dq ** -0.5
    kpos = jnp.arange(T)

    def block(i):
        start = i * QB
        qb = lax.dynamic_slice_in_dim(q, start, QB, axis=1)
        s = jnp.einsum('bqhd,bkhd->bhqk', qb, k, preferred_element_type=jnp.float32) * scale
        if fcum is not None:
            fq = lax.dynamic_slice_in_dim(fcum, start, QB, axis=2)
            s = s + fq[..., :, None] - fcum[..., None, :]
        qpos = start + jnp.arange(QB)
        mask = (kpos[None, :] <= qpos[:, None]) & valid[None, :]
        p = jax.nn.softmax(jnp.where(mask, s, NEG), axis=-1)
        return jnp.einsum('bhqk,bkhd->bqhd', p.astype(v.dtype), v)

    o = lax.map(block, jnp.arange(T // QB))
    return jnp.moveaxis(o, 0, 1).reshape(B, T, H, v.shape[-1])


def _joint_softmax(s_past, s_new):
    L = s_new.shape[-1]
    tri = jnp.tril(jnp.ones((L, L), dtype=bool))
    s = jnp.concatenate([s_past, jnp.where(tri, s_new, NEG)], axis=-1)
    p = jax.nn.softmax(s, axis=-1)
    P = s_past.shape[-1]
    return p[..., :P], p[..., P:]


def _merge(h, outs, zs, gates, w_br, w_out):
    offs = np.cumsum((0,) + BRANCH_WIDTHS).tolist()
    acc = None
    for m in range(len(outs)):
        o = outs[m].reshape(outs[m].shape[:2] + (-1,)) * jax.nn.silu(zs[m])
        term = jax.nn.sigmoid(gates[m]) * jnp.einsum('btw,wd->btd', o, w_br[offs[m]:offs[m + 1]])
        acc = term if acc is None else acc + term
    return h + jnp.einsum('btd,de->bte', acc, w_out)


def _prompt_layer(h, valid, pos, pad_front, lp):
    B, T, _ = h.shape
    (a_cq, a_ckv, a_kr, f_q, f_k, f_v, f_f, c_u, c_b, c_c, r_q, r_k, r_v,
     z_a, z_b, z_c, z_d, g_a, g_b, g_c, g_d) = _split_in(_rmsnorm(h, lp['g_norm']), lp['w_in'])
    q, ckv, kr = _mla_project(a_cq, a_ckv, a_kr, pos, lp)
    kn = _mla_key_nope(ckv, lp)
    k = jnp.concatenate([kn, jnp.broadcast_to(kr[:, :, None, :], (B, T, H_A, ROPE_DIM))], axis=-1)
    v = jnp.einsum('btr,rhd->bthd', ckv, lp['w_uv'].reshape(KV_RANK, H_A, HD))
    o_a = _sweep_attention(q, k, v, valid, None)
    fq, fk, fv, logf = _fox_project(f_q, f_k, f_v, f_f, lp)
    fcum = jnp.cumsum(logf, axis=1).transpose(0, 2, 1)
    o_b = _sweep_attention(fq, fk, fv, valid, fcum)
    u = jnp.where(valid[None, :, None], c_c * c_u, 0.0)
    up = jnp.pad(u, ((0, 0), (CONV_W - 1, 0), (0, 0)))
    o_c = c_b * _short_conv(up, lp['w_conv'])
    rq, rk, rv = _ret_project(r_q, r_k, r_v, pos)
    vm = valid[None, :, None, None]
    rk = jnp.where(vm, rk, 0.0)
    rv = jnp.where(vm, rv, 0.0)
    nc = T // QB

    def to_chunks(a):
        return jnp.moveaxis(a.reshape((B, nc, QB) + a.shape[2:]), 1, 0)

    s0 = jnp.zeros((B, H_R, DK_R, DV_R), h.dtype)
    s_fin, o_d = lax.scan(lambda s, c: _retention_chunk(c[0], c[1], c[2], s), s0,
                          (to_chunks(rq), to_chunks(rk), to_chunks(rv)))
    o_d = _rmsnorm(jnp.moveaxis(o_d, 0, 1).reshape(B, T, H_R, DV_R), lp['g_ret'])
    h = _merge(h, (o_a, o_b, o_c, o_d), (z_a, z_b, z_c, z_d), (g_a, g_b, g_c, g_d), lp['w_br'], lp['w_out'])
    pf = pad_front
    state = (ckv[:, pf:], kr[:, pf:], fk[:, pf:], fv[:, pf:], logf[:, pf:].astype(h.dtype),
             u[:, T - (CONV_W - 1):], s_fin)
    return h, state


def _sample_layer(h, pos, page_table, ckv_pool, kr_pool, fk_pool, fv_pool, flf_pool, conv_st, ret_st, lp):
    DB, L, _ = h.shape
    dt = h.dtype
    (a_cq, a_ckv, a_kr, f_q, f_k, f_v, f_f, c_u, c_b, c_c, r_q, r_k, r_v,
     z_a, z_b, z_c, z_d, g_a, g_b, g_c, g_d) = _split_in(_rmsnorm(h, lp['g_norm']), lp['w_in'])
    q, ckv, kr = _mla_project(a_cq, a_ckv, a_kr, pos, lp)
    qn, qr = q[..., :HD], q[..., HD:]
    ckv_p = _gather_pages(ckv_pool, page_table)
    kr_p = _gather_pages(kr_pool, page_table)
    scale_a = (HD + ROPE_DIM) ** -0.5

    def mla_scores(ck, krk):
        return (jnp.einsum('blhd,bshd->bhls', qn, _mla_key_nope(ck, lp), preferred_element_type=jnp.float32)
                + jnp.einsum('blhd,bsd->bhls', qr, krk, preferred_element_type=jnp.float32)) * scale_a

    p_p, p_n = _joint_softmax(mla_scores(ckv_p, kr_p), mla_scores(ckv, kr))
    lat = (jnp.einsum('bhls,bsr->blhr', p_p.astype(dt), ckv_p)
           + jnp.einsum('bhls,bsr->blhr', p_n.astype(dt), ckv))
    o_a = jnp.einsum('blhr,rhd->blhd', lat, lp['w_uv'].reshape(KV_RANK, H_A, HD))
    fq, fk, fv, logf = _fox_project(f_q, f_k, f_v, f_f, lp)
    fk_p = _gather_pages(fk_pool, page_table)
    fv_p = _gather_pages(fv_pool, page_table)
    lf_p = _gather_pages(flf_pool, page_table).astype(jnp.float32)
    suffix = jnp.flip(jnp.cumsum(jnp.flip(lf_p, axis=1), axis=1), axis=1) - lf_p
    fnew = jnp.cumsum(logf, axis=1).transpose(0, 2, 1)
    scale_b = HD ** -0.5
    s_past = (jnp.einsum('blhd,bshd->bhls', fq, fk_p, preferred_element_type=jnp.float32) * scale_b
              + fnew[..., :, None] + suffix.transpose(0, 2, 1)[:, :, None, :])
    s_new = (jnp.einsum('blhd,bshd->bhls', fq, fk, preferred_element_type=jnp.float32) * scale_b
             + fnew[..., :, None] - fnew[..., None, :])
    p_p, p_n = _joint_softmax(s_past, s_new)
    o_b = (jnp.einsum('bhls,bshd->blhd', p_p.astype(dt), fv_p)
           + jnp.einsum('bhls,bshd->blhd', p_n.astype(dt), fv))
    up = jnp.concatenate([conv_st.astype(dt), c_c * c_u], axis=1)
    o_c = c_b * _short_conv(up, lp['w_conv'])
    rq, rk, rv = _ret_project(r_q, r_k, r_v, pos)
    s_new_ret, o_d = _retention_chunk(rq, rk, rv, ret_st.astype(dt))
    o_d = _rmsnorm(o_d, lp['g_ret'])
    h = _merge(h, (o_a, o_b, o_c, o_d), (z_a, z_b, z_c, z_d), (g_a, g_b, g_c, g_d), lp['w_br'], lp['w_out'])
    state = (ckv, kr, fk, fv, logf.astype(dt), up[:, up.shape[1] - (CONV_W - 1):], s_new_ret)
    return h, state


def setup_inputs(seed: int = 0) -> dict:
    key = jax.random.key(seed)
    ks = iter(jax.random.split(key, 40))
    n_pages = PAST_LEN // PAGE_SIZE
    n_used = DEC_BATCH * n_pages
    n_phys = n_used + max(1, n_used // 4)

    def nrm(shape, s=1.0):
        return s * jax.random.normal(next(ks), shape, jnp.float32)

    def gain(shape):
        return 1.0 + nrm(shape, 0.1)

    w_tot = sum(BRANCH_WIDTHS)
    br_scale = jnp.asarray(np.repeat(np.array([w ** -0.5 for w in BRANCH_WIDTHS], np.float32),
                                     np.array(BRANCH_WIDTHS)))
    page_table = jax.random.permutation(next(ks), n_phys)[:n_used].reshape(DEC_BATCH, n_pages).astype(jnp.int32)
    x_prompt = nrm((BATCH, SEQ, D_MODEL))
    x_sample = nrm((DEC_BATCH, DEC_SEQ, D_MODEL))
    cache_mla_ckv = nrm((DEPTH, n_phys, PAGE_SIZE, KV_RANK))
    cache_mla_krope = nrm((DEPTH, n_phys, PAGE_SIZE, ROPE_DIM))
    cache_fox_k = nrm((DEPTH, n_phys, PAGE_SIZE, H_F, HD))
    cache_fox_v = nrm((DEPTH, n_phys, PAGE_SIZE, H_F, HD))
    cache_fox_logf = jax.nn.log_sigmoid(2.0 + nrm((DEPTH, n_phys, PAGE_SIZE, H_F)))
    state_conv = nrm((DEPTH, DEC_BATCH, CONV_W - 1, W_C))
    state_ret = nrm((DEPTH, DEC_BATCH, H_R, DK_R, DV_R), 0.5)
    return {
        'x_prompt': x_prompt,
        'x_sample': x_sample,
        'cache_mla_ckv': cache_mla_ckv,
        'cache_mla_krope': cache_mla_krope,
        'cache_fox_k': cache_fox_k,
        'cache_fox_v': cache_fox_v,
        'cache_fox_logf': cache_fox_logf,
        'state_conv': state_conv,
        'state_ret': state_ret,
        'page_table': page_table,
        'meta': nrm((N_META, D_MODEL)),
        'g_norm': gain((DEPTH, D_MODEL)),
        'w_in': nrm((DEPTH, D_MODEL, N_IN), D_MODEL ** -0.5),
        'b_f': 2.0 + nrm((DEPTH, H_F), 0.1),
        'g_cq': gain((DEPTH, Q_RANK)),
        'g_ckv': gain((DEPTH, KV_RANK)),
        'w_uq': nrm((DEPTH, Q_RANK, H_A * (HD + ROPE_DIM)), Q_RANK ** -0.5),
        'w_uk': nrm((DEPTH, KV_RANK, H_A * HD), KV_RANK ** -0.5),
        'w_uv': nrm((DEPTH, KV_RANK, H_A * HD), KV_RANK ** -0.5),
        'g_mla_qn': gain((DEPTH, HD)),
        'g_mla_qr': gain((DEPTH, ROPE_DIM)),
        'g_mla_kn': gain((DEPTH, HD)),
        'g_mla_kr': gain((DEPTH, ROPE_DIM)),
        'g_fox_q': gain((DEPTH, HD)),
        'g_fox_k': gain((DEPTH, HD)),
        'w_conv': nrm((DEPTH, CONV_W, W_C), CONV_W ** -0.5),
        'g_ret': gain((DEPTH, DV_R)),
        'w_br': nrm((DEPTH, w_tot, D_MODEL)) * br_scale[None, :, None],
        'w_out': nrm((DEPTH, D_MODEL, D_MODEL), D_MODEL ** -0.5),
    }


def reference(x_prompt, x_sample, cache_mla_ckv, cache_mla_krope, cache_fox_k, cache_fox_v,
              cache_fox_logf, state_conv, state_ret, page_table, meta, g_norm, w_in, b_f,
              g_cq, g_ckv, w_uq, w_uk, w_uv, g_mla_qn, g_mla_qr, g_mla_kn, g_mla_kr,
              g_fox_q, g_fox_k, w_conv, g_ret, w_br, w_out):
    dt = x_prompt.dtype
    B, S, _ = x_prompt.shape
    L = x_sample.shape[1]
    pad_front = (-N_META) % QB
    t_pad = pad_front + N_META + S
    h_p = jnp.concatenate([jnp.zeros((B, pad_front, D_MODEL), dt),
                           jnp.broadcast_to(meta.astype(dt)[None], (B, N_META, D_MODEL)),
                           x_prompt], axis=1)
    valid = jnp.arange(t_pad) >= pad_front
    pos_p = jnp.arange(t_pad) - pad_front
    pos_s = PAST_LEN + jnp.arange(L)
    h_s = x_sample
    new_p = [[] for _ in range(7)]
    new_s = [[] for _ in range(7)]
    for l in range(DEPTH):
        lp = {'g_norm': g_norm[l], 'w_in': w_in[l], 'b_f': b_f[l], 'g_cq': g_cq[l], 'g_ckv': g_ckv[l],
              'w_uq': w_uq[l], 'w_uk': w_uk[l], 'w_uv': w_uv[l], 'g_qn': g_mla_qn[l], 'g_qr': g_mla_qr[l],
              'g_kn': g_mla_kn[l], 'g_kr': g_mla_kr[l], 'g_fq': g_fox_q[l], 'g_fk': g_fox_k[l],
              'w_conv': w_conv[l], 'g_ret': g_ret[l], 'w_br': w_br[l], 'w_out': w_out[l]}
        h_p, st_p = _prompt_layer(h_p, valid, pos_p, pad_front, lp)
        h_s, st_s = _sample_layer(h_s, pos_s, page_table, cache_mla_ckv[l], cache_mla_krope[l],
                                  cache_fox_k[l], cache_fox_v[l], cache_fox_logf[l],
                                  state_conv[l], state_ret[l], lp)
        for j in range(7):
            new_p[j].append(st_p[j])
            new_s[j].append(st_s[j])
    p_ckv, p_kr, p_fk, p_fv, p_flf, p_conv, p_ret = [jnp.stack(a) for a in new_p]
    s_ckv, s_kr, s_fk, s_fv, s_flf, s_conv, s_ret = [jnp.stack(a) for a in new_s]
    y_prompt = h_p[:, pad_front + N_META:]
    y_sample = h_s
    return (y_prompt, y_sample, p_ckv, p_kr, p_fk, p_fv, p_flf, p_conv, p_ret,
            s_ckv, s_kr, s_fk, s_fv, s_flf, s_conv, s_ret)
```

```python
import functools

import numpy as np
import jax
import jax.numpy as jnp
from jax import lax
from jax.experimental import pallas as pl
from jax.experimental.pallas import tpu as pltpu

D_MODEL = 1024
N_META = 16
HD = 64
H_A = 8
Q_RANK = 256
KV_RANK = 256
ROPE_DIM = 32
H_F = 8
W_C = 512
CONV_W = 3
H_R = 8
DK_R = 64
DV_R = 128
QB = 128
PAGE = 128
ROPE_BASE = 10000.0
EPS = 1e-6
NEG = -1e30

LANES = 128
SLOT = 128
N_PAIR = 4
VMEM_LIMIT = 48 * 1024 * 1024

F32 = jnp.float32
BF = jnp.bfloat16

_ORIG = (("a_cq", 256), ("a_ckv", 256), ("a_kr", 32), ("f_q", 512), ("f_k", 512), ("f_v", 512), ("f_f", 8),
         ("c_u", 512), ("c_b", 512), ("c_c", 512), ("r_q", 512), ("r_k", 512), ("r_v", 1024),
         ("z_a", 512), ("z_b", 512), ("z_c", 512), ("z_d", 1024),
         ("g_a", 1024), ("g_b", 1024), ("g_c", 1024), ("g_d", 1024))
_ORDER = ("g_a", "g_b", "g_c", "g_d", "r_v", "z_d", "f_q", "f_k", "f_v", "c_u", "c_b", "c_c", "r_q", "r_k",
          "z_a", "z_b", "z_c", "a_cq", "a_ckv", "a_kr", "f_f")


def _layout():
    src, off = {}, 0
    for name, w in _ORIG:
        src[name] = (off, w)
        off += w
    seg, off = {}, 0
    for name in _ORDER:
        w = src[name][1]
        wp = -(-w // LANES) * LANES
        assert off % wp == 0
        seg[name] = (off, wp)
        off += wp
    return src, seg, off


_SRC, _SEG, N_P = _layout()


def _cb(name):
    off, w = _SEG[name]
    return off // w


def _pick(n, prefs):
    for t in prefs:
        if n % t == 0:
            return t
    return n


def _dot(a, b):
    return jnp.dot(a, b, preferred_element_type=F32)


def _dot_nt(a, b):
    return lax.dot_general(a, b, (((1,), (1,)), ((), ())), preferred_element_type=F32)


def _dot_tn(a, b):
    return lax.dot_general(a, b, (((0,), (0,)), ((), ())), preferred_element_type=F32)


def _split2_dot(x, m):
    hi = x.astype(BF)
    lo = (x - hi.astype(F32)).astype(BF)
    return _dot(hi, m) + _dot(lo, m)


def _split3(x):
    hi = x.astype(BF)
    r1 = x - hi.astype(F32)
    mid = r1.astype(BF)
    lo = (r1 - mid.astype(F32)).astype(BF)
    return hi, mid, lo


def _rms(x, g, n):
    ms = jnp.sum(x * x, axis=-1, keepdims=True) * (1.0 / n)
    return x * lax.rsqrt(ms + EPS) * g


def _params(*sem):
    return pltpu.CompilerParams(dimension_semantics=sem, vmem_limit_bytes=VMEM_LIMIT)


def _proj_kernel(x_ref, g_ref, w_ref, o_ref, xn_ref):
    @pl.when(pl.program_id(1) == 0)
    def _():
        xn_ref[...] = _rms(x_ref[...], g_ref[...], D_MODEL).astype(BF)

    o_ref[...] = _dot(xn_ref[...], w_ref[...])


def _proj(x, g, w):
    m = x.shape[0]
    tm = _pick(m, (1280, 640, 512, 384, 256, 128))
    tn = _pick(N_P, (896, 128))
    return pl.pallas_call(
        _proj_kernel,
        grid=(m // tm, N_P // tn),
        in_specs=[pl.BlockSpec((tm, D_MODEL), lambda i, j: (i, 0)),
                  pl.BlockSpec((1, D_MODEL), lambda i, j: (0, 0)),
                  pl.BlockSpec((D_MODEL, tn), lambda i, j: (0, j))],
        out_specs=pl.BlockSpec((tm, tn), lambda i, j: (i, j)),
        out_shape=jax.ShapeDtypeStruct((m, N_P), F32),
        scratch_shapes=[pltpu.VMEM((tm, D_MODEL), BF)],
        compiler_params=_params("parallel", "arbitrary"),
        name="proj",
    )(x, g, w)


def _rot_slot(x, cos, sin, first_half_end, half):
    w = x.shape[-1]
    lane = lax.broadcasted_iota(jnp.int32, x.shape, 1) % LANES
    partner = jnp.where(lane < first_half_end, pltpu.roll(x, w - half, 1), pltpu.roll(x, half, 1))
    return x * cos + partner * sin


def _mla_prep_kernel(cq_ref, ckv_ref, kr_ref, cosq_ref, sinq_ref, cosk_ref, sink_ref,
                     gcq_ref, wuq_ref, mq_ref, gq_ref, qmul_ref, gckv_ref, gkr_ref,
                     wuk_ref, gk_ref, place_ref, wuv_ref,
                     q_ref, ckvn_ref, krr_ref, *kv_refs):
    cqn = _rms(cq_ref[...], gcq_ref[...], Q_RANK).astype(BF)
    qraw = _dot(cqn, wuq_ref[...])
    qn = qraw * lax.rsqrt(_split2_dot(qraw * qraw, mq_ref[...]) + EPS) * gq_ref[...]
    cosq = jnp.tile(cosq_ref[...], (1, H_A))
    sinq = jnp.tile(sinq_ref[...], (1, H_A))
    q = _rot_slot(qn, cosq, sinq, HD + ROPE_DIM // 2, ROPE_DIM // 2)
    q_ref[...] = (q * qmul_ref[...]).astype(BF)

    ckvn = _rms(ckv_ref[...], gckv_ref[...], KV_RANK)
    ckvn_ref[...] = ckvn
    krn = _rms(kr_ref[...], gkr_ref[...], ROPE_DIM)
    krr = _rot_slot(krn, cosk_ref[...], sink_ref[...], ROPE_DIM // 2, ROPE_DIM // 2)
    krr_ref[...] = krr

    if kv_refs:
        k_ref, v_ref = kv_refs
        ckvb = ckvn.astype(BF)
        knraw = _dot(ckvb, wuk_ref[...])
        kn = knraw * lax.rsqrt(_split2_dot(knraw * knraw, mq_ref[...]) + EPS) * gk_ref[...]
        k_ref[...] = (kn + _dot(krr.astype(BF), place_ref[...])).astype(BF)
        v_ref[...] = _dot(ckvb, wuv_ref[...]).astype(BF)


def _mla_prep(p, tabs, wts, with_kv):
    m = p.shape[0]
    tm = _pick(m, (256, 128))
    row = lambda w, c: pl.BlockSpec((tm, w), lambda i, c=c: (i, c))
    full = lambda a: pl.BlockSpec(a.shape, lambda i: (0,) * a.ndim)
    out_shape = [jax.ShapeDtypeStruct((m, H_A * SLOT), BF),
                 jax.ShapeDtypeStruct((m, KV_RANK), F32),
                 jax.ShapeDtypeStruct((m, LANES), F32)]
    out_specs = [row(H_A * SLOT, 0), row(KV_RANK, 0), row(LANES, 0)]
    if with_kv:
        out_shape += [jax.ShapeDtypeStruct((m, H_A * SLOT), BF), jax.ShapeDtypeStruct((m, H_A * HD), BF)]
        out_specs += [row(H_A * SLOT, 0), row(H_A * HD, 0)]
    return pl.pallas_call(
        _mla_prep_kernel,
        grid=(m // tm,),
        in_specs=[row(256, _cb("a_cq")), row(256, _cb("a_ckv")), row(LANES, _cb("a_kr"))]
                 + [row(LANES, 0)] * 4 + [full(a) for a in wts],
        out_specs=out_specs,
        out_shape=out_shape,
        compiler_params=_params("parallel"),
        name="mla_prep",
    )(p, p, p, *tabs, *wts)


def _fox_prep_kernel(fq_ref, fk_ref, fv_ref, ff_ref, gq_ref, gk_ref, m64_ref, bf_ref,
                     q_ref, k_ref, kb_ref, vb_ref, logf_ref):
    fq = fq_ref[...]
    fk = fk_ref[...]
    m64 = m64_ref[...]
    q_ref[...] = (fq * lax.rsqrt(_split2_dot(fq * fq, m64) + EPS) * gq_ref[...]).astype(BF)
    kn = fk * lax.rsqrt(_split2_dot(fk * fk, m64) + EPS) * gk_ref[...]
    k_ref[...] = kn
    kb_ref[...] = kn.astype(BF)
    vb_ref[...] = fv_ref[...].astype(BF)
    x = ff_ref[...] + bf_ref[...]
    logf_ref[...] = jnp.minimum(x, 0.0) - jnp.log1p(jnp.exp(-jnp.abs(x)))


def _fox_prep(p, wts):
    m = p.shape[0]
    tm = _pick(m, (256, 128))
    row = lambda w, c: pl.BlockSpec((tm, w), lambda i, c=c: (i, c))
    full = lambda a: pl.BlockSpec(a.shape, lambda i: (0,) * a.ndim)
    w = H_F * HD
    return pl.pallas_call(
        _fox_prep_kernel,
        grid=(m // tm,),
        in_specs=[row(w, _cb("f_q")), row(w, _cb("f_k")), row(w, _cb("f_v")), row(LANES, _cb("f_f"))]
                 + [full(a) for a in wts],
        out_specs=[row(w, 0), row(w, 0), row(w, 0), row(w, 0), row(LANES, 0)],
        out_shape=[jax.ShapeDtypeStruct((m, w), BF), jax.ShapeDtypeStruct((m, w), F32),
                   jax.ShapeDtypeStruct((m, w), BF), jax.ShapeDtypeStruct((m, w), BF),
                   jax.ShapeDtypeStruct((m, LANES), F32)],
        compiler_params=_params("parallel"),
        name="fox_prep",
    )(p, p, p, p, *wts)


def _seq_prep_kernel(logf_ref, cu_ref, cb_ref, cc_ref, tri_ref, wc_ref,
                     fcum_ref, oc_ref, st_ref, carry_ref, ubuf_ref, *, tm, pf):
    t = pl.program_id(1)
    rows = t * tm + lax.broadcasted_iota(jnp.int32, (tm, 1), 0)
    valid = rows >= pf

    @pl.when(t == 0)
    def _():
        carry_ref[...] = jnp.zeros_like(carry_ref)
        ubuf_ref[0:8, :] = jnp.zeros((8, W_C), F32)

    @pl.when(t > 0)
    def _():
        ubuf_ref[0:8, :] = ubuf_ref[tm:tm + 8, :]

    lf = jnp.where(valid, logf_ref[...], 0.0)
    tri = tri_ref[...]
    hi, mid, lo = _split3(lf)
    cs = _dot(tri, hi) + _dot(tri, mid) + _dot(tri, lo) + carry_ref[0:1, :]
    fcum_ref[...] = cs
    carry_ref[...] = jnp.broadcast_to(cs[tm - 1:tm, :], carry_ref.shape)

    u = jnp.where(valid, cc_ref[...] * cu_ref[...], 0.0)
    ubuf_ref[8:8 + tm, :] = u
    wc = wc_ref[...]
    y = wc[0:1, :] * ubuf_ref[6:6 + tm, :] + wc[1:2, :] * ubuf_ref[7:7 + tm, :] + wc[2:3, :] * u
    oc_ref[...] = cb_ref[...] * y
    st_ref[...] = ubuf_ref[tm + 6:tm + 8, :]


def _seq_prep(p, logf, tri, wc, nb, tp, pf):
    tm = tri.shape[0]
    nt = tp // tm
    row = lambda w, c: pl.BlockSpec((tm, w), lambda b, t, c=c: (b * nt + t, c))
    full = lambda a: pl.BlockSpec(a.shape, lambda b, t: (0,) * a.ndim)
    return pl.pallas_call(
        functools.partial(_seq_prep_kernel, tm=tm, pf=pf),
        grid=(nb, nt),
        in_specs=[row(LANES, 0), row(W_C, _cb("c_u")), row(W_C, _cb("c_b")), row(W_C, _cb("c_c")),
                  full(tri), full(wc)],
        out_specs=[row(LANES, 0), row(W_C, 0),
                   pl.BlockSpec((None, CONV_W - 1, W_C), lambda b, t: (b, 0, 0))],
        out_shape=[jax.ShapeDtypeStruct((nb * tp, LANES), F32), jax.ShapeDtypeStruct((nb * tp, W_C), F32),
                   jax.ShapeDtypeStruct((nb, CONV_W - 1, W_C), F32)],
        scratch_shapes=[pltpu.VMEM((8, LANES), F32), pltpu.VMEM((tm + 8, W_C), F32)],
        compiler_params=_params("parallel", "arbitrary"),
        name="seq_prep",
    )(logf, p, p, p, tri, wc)


def _flash_kernel(q_ref, k_ref, v_ref, *rest, tq, tk, pf, has_bias):
    if has_bias:
        b_ref, o_ref, m0_ref, m1_ref, l0_ref, l1_ref, acc_ref = rest
    else:
        o_ref, m0_ref, m1_ref, l0_ref, l1_ref, acc_ref = rest
    i = pl.program_id(2)
    a = tq // tk
    wq = q_ref.shape[-1]
    q = q_ref[...]
    lane_q = lax.broadcasted_iota(jnp.int32, (tq, wq), 1)
    zero = jnp.zeros_like(q)
    qh = (jnp.where(lane_q < wq // 2, q, zero), jnp.where(lane_q >= wq // 2, q, zero))
    lane_v = lax.broadcasted_iota(jnp.int32, (tq, LANES), 1)
    first = lane_v < HD

    for r in (m0_ref, m1_ref):
        r[...] = jnp.full(r.shape, NEG, F32)
    for r in (l0_ref, l1_ref, acc_ref):
        r[...] = jnp.zeros(r.shape, F32)

    def step(c, masked):
        start = pl.multiple_of(c * tk, tk)
        k = k_ref[pl.ds(start, tk), :]
        v = v_ref[pl.ds(start, tk), :]
        if masked:
            kpos = start + lax.broadcasted_iota(jnp.int32, (tq, tk), 1)
            qpos = i * tq + lax.broadcasted_iota(jnp.int32, (tq, tk), 0)
            mask = (kpos <= qpos) & (kpos >= pf)
        alphas, pvs = [], []
        for hh, (m_ref, l_ref) in enumerate(((m0_ref, l0_ref), (m1_ref, l1_ref))):
            s = _dot_nt(qh[hh], k)
            if has_bias:
                s = s - b_ref[hh:hh + 1, pl.ds(start, tk)]
            if masked:
                s = jnp.where(mask, s, NEG)
            m_prev = m_ref[...]
            m_new = jnp.maximum(m_prev, jnp.max(s, axis=1, keepdims=True))
            alpha = jnp.exp(m_prev - m_new)
            p = jnp.exp(s - m_new)
            l_ref[...] = alpha * l_ref[...] + jnp.sum(p, axis=1, keepdims=True)
            m_ref[...] = m_new
            alphas.append(alpha)
            pvs.append(_dot(p.astype(BF), v))
        acc_ref[...] = acc_ref[...] * jnp.where(first, alphas[0], alphas[1]) + jnp.where(first, pvs[0], pvs[1])

    step(0, True)

    def body(c, carry):
        step(c, False)
        return carry

    lax.fori_loop(1, i * a, body, 0)
    for d in range(a):
        if d == 0:
            @pl.when(i > 0)
            def _():
                step(i * a, True)
        else:
            step(i * a + d, True)

    o_ref[...] = acc_ref[...] / jnp.where(first, l0_ref[...], l1_ref[...])


def _flash(q, k, v, bias, nb, tp, pf):
    wq = q.shape[1] // N_PAIR
    tk = QB
    tq = tk * _pick(tp // tk, (5, 3, 1))
    nq = tp // tq
    k3 = k.reshape(nb, tp, k.shape[1])
    v3 = v.reshape(nb, tp, v.shape[1])
    in_specs = [pl.BlockSpec((tq, wq), lambda b, h, i: (b * nq + i, h)),
                pl.BlockSpec((None, tp, wq), lambda b, h, i: (b, 0, h)),
                pl.BlockSpec((None, tp, LANES), lambda b, h, i: (b, 0, h))]
    args = [q, k3, v3]
    if bias is not None:
        in_specs.append(pl.BlockSpec((None, None, 2, tp), lambda b, h, i: (b, h, 0, 0)))
        args.append(bias)
    return pl.pallas_call(
        functools.partial(_flash_kernel, tq=tq, tk=tk, pf=pf, has_bias=bias is not None),
        grid=(nb, N_PAIR, nq),
        in_specs=in_specs,
        out_specs=pl.BlockSpec((tq, LANES), lambda b, h, i: (b * nq + i, h)),
        out_shape=jax.ShapeDtypeStruct((nb * tp, N_PAIR * LANES), F32),
        scratch_shapes=[pltpu.VMEM((tq, LANES), F32)] * 5,
        compiler_params=_params("parallel", "parallel", "arbitrary"),
        name="flash_bias" if bias is not None else "flash",
    )(*args)


def _ret_kernel(rq_ref, rk_ref, rv_ref, cos_ref, sin_ref, dec_ref, qdec_ref, kdec_ref, gc_ref, gret_ref, s0_ref,
                o_ref, sout_ref, s_ref, *, c, pf):
    ci = pl.program_id(1)

    @pl.when(ci == 0)
    def _():
        s_ref[...] = s0_ref[...]

    cos = jnp.tile(cos_ref[...], (1, H_R // 2))
    sin = jnp.tile(sin_ref[...], (1, H_R // 2))
    lane = lax.broadcasted_iota(jnp.int32, (c, H_R * DK_R), 1) % DK_R
    w = H_R * DK_R

    def rot(x):
        partner = jnp.where(lane < DK_R // 2, pltpu.roll(x, w - DK_R // 2, 1), pltpu.roll(x, DK_R // 2, 1))
        return x * cos + partner * sin

    rows = ci * c + lax.broadcasted_iota(jnp.int32, (c, 1), 0)
    valid = rows >= pf
    q = rot(rq_ref[...]).astype(BF)
    k = jnp.where(valid, rot(rk_ref[...]) * (DK_R ** -0.5), 0.0)
    kd = (k * kdec_ref[...]).astype(BF)
    k = k.astype(BF)
    v = jnp.where(valid, rv_ref[...], 0.0).astype(BF)
    if c < QB:
        padrows = lambda x: jnp.concatenate([x, jnp.zeros((QB - c, x.shape[1]), x.dtype)], axis=0)
        k, kd, v = padrows(k), padrows(kd), padrows(v)
    lane_p = lax.broadcasted_iota(jnp.int32, (c, LANES), 1)
    row_p = lax.broadcasted_iota(jnp.int32, (LANES, DV_R), 0)
    zq = jnp.zeros((c, LANES), BF)
    for pp in range(H_R // 2):
        sl = slice(pp * LANES, (pp + 1) * LANES)
        qp, kp, kdp = q[:, sl], k[:, sl], kd[:, sl]
        s_pair = s_ref[sl, :]
        s_b = s_pair.astype(BF)
        upd = []
        for hh in range(2):
            h = 2 * pp + hh
            vh = v[:, h * DV_R:(h + 1) * DV_R]
            qm = jnp.where((lane_p < DK_R) == (hh == 0), qp, zq)
            qk = (_dot_nt(qm, kp) * dec_ref[h]).astype(BF)
            o = _dot(qk, vh) + _dot(qm, s_b) * qdec_ref[h]
            o_ref[:, h * DV_R:(h + 1) * DV_R] = _rms(o, gret_ref[...], DV_R)
            upd.append(_dot_tn(kdp, vh))
        s_ref[sl, :] = gc_ref[sl, :] * s_pair + jnp.where(row_p < DK_R, upd[0], upd[1])
    sout_ref[...] = s_ref[...]


def _retention(p, tabs, gret, s0, nb, nc, c, pf, row_block):
    cos, sin, dec, qdec, kdec, gc = tabs
    if p.ndim == 2:
        row = lambda w, col: pl.BlockSpec((c, w), lambda b, t, col=col: (b * nc + t, col))
        tab = lambda a: pl.BlockSpec((c, a.shape[1]), lambda b, t: (t, 0))
        out_o = pl.BlockSpec((c, H_R * DV_R), lambda b, t: (b * nc + t, 0))
        o_shape = jax.ShapeDtypeStruct((nb * nc * c, H_R * DV_R), F32)
    else:
        row = lambda w, col: pl.BlockSpec((None, c, w), lambda b, t, col=col: (b, 0, col))
        tab = lambda a: pl.BlockSpec((c, a.shape[1]), lambda b, t: (0, 0))
        out_o = pl.BlockSpec((None, c, H_R * DV_R), lambda b, t: (b, 0, 0))
        o_shape = jax.ShapeDtypeStruct((nb, c, H_R * DV_R), F32)
    full = lambda a: pl.BlockSpec(a.shape, lambda b, t: (0,) * a.ndim)
    st = pl.BlockSpec((None, H_R * DK_R, DV_R), lambda b, t: (b, 0, 0))
    return pl.pallas_call(
        functools.partial(_ret_kernel, c=c, pf=pf),
        grid=(nb, nc),
        in_specs=[row(H_R * DK_R, _cb("r_q")), row(H_R * DK_R, _cb("r_k")), row(H_R * DV_R, _cb("r_v")),
                  tab(cos), tab(sin), full(dec), full(qdec), full(kdec), full(gc), full(gret), st],
        out_specs=[out_o, st],
        out_shape=[o_shape, jax.ShapeDtypeStruct((nb, H_R * DK_R, DV_R), F32)],
        scratch_shapes=[pltpu.VMEM((H_R * DK_R, DV_R), F32)],
        compiler_params=_params("parallel", "arbitrary"),
        name="retention",
    )(p, p, p, cos, sin, dec, qdec, kdec, gc, gret, s0)


def _merge_kernel(h_ref, oa_ref, ob_ref, oc_ref, od_ref, za_ref, zb_ref, zc_ref, zd_ref,
                  ga_ref, gb_ref, gc_ref, gd_ref, wa_ref, wb_ref, wc_ref, wd_ref, wo_ref, out_ref):
    acc = None
    for o_ref, z_ref, g_ref, w_ref in ((oa_ref, za_ref, ga_ref, wa_ref), (ob_ref, zb_ref, gb_ref, wb_ref),
                                       (oc_ref, zc_ref, gc_ref, wc_ref), (od_ref, zd_ref, gd_ref, wd_ref)):
        z = z_ref[...]
        x = (o_ref[...] * (z * jax.nn.sigmoid(z))).astype(BF)
        term = jax.nn.sigmoid(g_ref[...]) * _dot(x, w_ref[...])
        acc = term if acc is None else acc + term
    out_ref[...] = h_ref[...] + _dot(acc.astype(BF), wo_ref[...])


def _merge(h, outs, p, wbr, wo):
    m = h.shape[0]
    tm = _pick(m, (256, 128))
    row = lambda w, c: pl.BlockSpec((tm, w), lambda i, c=c: (i, c))
    full = lambda a: pl.BlockSpec(a.shape, lambda i: (0,) * a.ndim)
    widths = (512, 512, 512, 1024)
    return pl.pallas_call(
        _merge_kernel,
        grid=(m // tm,),
        in_specs=[row(D_MODEL, 0)] + [row(w, 0) for w in widths]
                 + [row(w, _cb(n)) for w, n in zip(widths, ("z_a", "z_b", "z_c", "z_d"))]
                 + [row(D_MODEL, _cb(n)) for n in ("g_a", "g_b", "g_c", "g_d")]
                 + [full(a) for a in wbr] + [full(wo)],
        out_specs=row(D_MODEL, 0),
        out_shape=jax.ShapeDtypeStruct((m, D_MODEL), F32),
        compiler_params=_params("parallel"),
        name="merge",
    )(h, *outs, p, p, p, p, p, p, p, p, *wbr, wo)


def _sample_local_kernel(cu_ref, cb_ref, cc_ref, logf_ref, st_ref, wc_ref,
                         oc_ref, stout_ref, fnew_ref, up_ref, *, n_new):
    row = lax.broadcasted_iota(jnp.int32, (8, 1), 0)
    u = cc_ref[...] * cu_ref[...]
    up_ref[0:8, :] = jnp.zeros((8, W_C), F32)
    up_ref[8 - (CONV_W - 1):8, :] = st_ref[...]
    up_ref[8:16, :] = u
    wc = wc_ref[...]
    y = wc[0:1, :] * up_ref[6:14, :] + wc[1:2, :] * up_ref[7:15, :] + wc[2:3, :] * u
    oc_ref[...] = cb_ref[...] * y
    stout_ref[...] = up_ref[8 + n_new - (CONV_W - 1):8 + n_new, :]
    lf = jnp.where(row < n_new, logf_ref[...], 0.0)
    cs = lf
    for j in range(1, n_new):
        cs = cs + jnp.where(row >= j, pltpu.roll(lf, j, 0), 0.0)
    fnew_ref[...] = cs


def _sample_local(p3, logf3, st, wc, n_new):
    nb = p3.shape[0]
    blk = lambda w, c: pl.BlockSpec((None, 8, w), lambda b, c=c: (b, 0, c))
    stb = pl.BlockSpec((None, CONV_W - 1, W_C), lambda b: (b, 0, 0))
    return pl.pallas_call(
        functools.partial(_sample_local_kernel, n_new=n_new),
        grid=(nb,),
        in_specs=[blk(W_C, _cb("c_u")), blk(W_C, _cb("c_b")), blk(W_C, _cb("c_c")), blk(LANES, 0), stb,
                  pl.BlockSpec(wc.shape, lambda b: (0, 0))],
        out_specs=[blk(W_C, 0), stb, blk(LANES, 0)],
        out_shape=[jax.ShapeDtypeStruct((nb, 8, W_C), F32), jax.ShapeDtypeStruct((nb, CONV_W - 1, W_C), F32),
                   jax.ShapeDtypeStruct((nb, 8, LANES), F32)],
        scratch_shapes=[pltpu.VMEM((16, W_C), F32)],
        compiler_params=_params("parallel"),
        name="sample_local",
    )(p3, p3, p3, logf3, st, wc)


def _online(s, m_ref, l_ref, accs):
    m_prev = m_ref[...]
    m_new = jnp.maximum(m_prev, jnp.max(s, axis=1, keepdims=True))
    alpha = jnp.exp(m_prev - m_new)
    p = jnp.exp(s - m_new)
    l_ref[...] = alpha * l_ref[...] + jnp.sum(p, axis=1, keepdims=True)
    m_ref[...] = m_new
    pb = p.astype(BF)
    for acc_ref, fn in accs:
        reps = acc_ref.shape[1] // LANES
        acc_ref[...] = acc_ref[...] * jnp.tile(alpha, (1, reps)) + fn(pb)


def _sample_attn_kernel(pt_ref, qn_ref, qr_ref, ckvn_ref, krn_ref, qf_ref, fkn_ref, fvn_ref, fcol_ref, frow_ref,
                        wuk_ref, wuv_ref, ind_ref, u_ref, bd_ref, *rest, pp, n_new):
    pages = rest[:5 * pp]
    oa_ref, ob_ref, ma_ref, la_ref, lat_ref, mf_ref, lf_ref, of_ref, car_ref = rest[5 * pp:]
    g = pl.program_id(1)
    rows = 8 * n_new
    qn = qn_ref[...]
    qr = qr_ref[...]
    qf = qf_ref[...]
    wuk = wuk_ref[...]
    ind = ind_ref[...]

    def mla_scores(ckvb, kr_scores):
        knraw = _dot(ckvb, wuk)
        sq = knraw * knraw
        hi = sq.astype(BF)
        lo = (sq - hi.astype(F32)).astype(BF)
        r = lax.rsqrt(_dot_nt(ind, hi) + _dot_nt(ind, lo) + EPS)
        return _dot_nt(qn, knraw.astype(BF)) * jnp.tile(r, (n_new, 1)) + kr_scores

    @pl.when(g == 0)
    def _():
        for r in (ma_ref, mf_ref):
            r[...] = jnp.full(r.shape, NEG, F32)
        for r in (la_ref, lat_ref, lf_ref, of_ref):
            r[...] = jnp.zeros(r.shape, F32)
        pad = lambda x: jnp.concatenate([x, jnp.zeros((PAGE - 8, x.shape[1]), x.dtype)], axis=0)
        qrow = lax.broadcasted_iota(jnp.int32, (rows, PAGE), 0) // 8
        kcol = lax.broadcasted_iota(jnp.int32, (rows, PAGE), 1)
        mask = kcol <= qrow
        ckvb = pad(ckvn_ref[...].astype(BF))
        s = mla_scores(ckvb, _dot_nt(qr, pad(krn_ref[...].astype(BF))))
        _online(jnp.where(mask, s, NEG), ma_ref, la_ref, [(lat_ref, lambda pb: _dot(pb, ckvb))])
        fk = pad(fkn_ref[...].astype(BF))
        fv = pad(fvn_ref[...].astype(BF))
        s = _dot_nt(qf, fk) + fcol_ref[...] - frow_ref[...]
        _online(jnp.where(mask, s, NEG), mf_ref, lf_ref, [(of_ref, lambda pb: _dot(pb, fv))])
        car_ref[...] = fcol_ref[...]

    for r in range(pp):
        ckv_ref, krt_ref, kt_ref, vt_ref, lft_ref = pages[5 * r:5 * r + 5]
        ckvb = ckv_ref[...].astype(BF)
        s = mla_scores(ckvb, _dot(qr, krt_ref[...].astype(BF)))
        _online(s, ma_ref, la_ref, [(lat_ref, lambda pb, ckvb=ckvb: _dot(pb, ckvb))])
        kt = kt_ref[...].reshape(H_F * HD, PAGE).astype(BF)
        vt = vt_ref[...].reshape(H_F * HD, PAGE).astype(BF)
        lft = lft_ref[...]
        hi, mid, lo = _split3(lft)
        u = u_ref[...]
        suffix = _dot(hi, u) + _dot(mid, u) + _dot(lo, u)
        s = _dot(qf, kt) + car_ref[...] + jnp.tile(suffix, (n_new, 1))
        _online(s, mf_ref, lf_ref, [(of_ref, lambda pb, vt=vt: _dot_nt(pb, vt))])
        car_ref[...] = car_ref[...] + jnp.tile(jnp.sum(lft, axis=1, keepdims=True), (n_new, 1))

    @pl.when(g == pl.num_programs(1) - 1)
    def _():
        bd = bd_ref[...]
        lat = (lat_ref[...] / jnp.tile(la_ref[...], (1, KV_RANK // LANES))).astype(BF)
        oa = _dot(lat, wuv_ref[...]) * bd
        oa_ref[...] = jnp.sum(oa.reshape(n_new, 8, H_A * HD), axis=1)
        ob = of_ref[...] / jnp.tile(lf_ref[...], (1, H_F * HD // LANES)) * bd
        ob_ref[...] = jnp.sum(ob.reshape(n_new, 8, H_F * HD), axis=1)


def _sample_attn(layer, pt, qn, qr, ckvn, krn, qf, fkn, fvn, fcol, frow, consts, pools, n_new):
    nb, n_pages = pt.shape
    pp = _pick(n_pages, (4, 2, 1))
    ng = n_pages // pp
    rows = 8 * n_new
    ckv_pool, krt_pool, kt_pool, vt_pool, lft_pool = pools
    per_b = lambda a: pl.BlockSpec((None,) + a.shape[1:], lambda b, g, pt: (b,) + (0,) * (a.ndim - 1))
    full = lambda a: pl.BlockSpec(a.shape, lambda b, g, pt: (0,) * a.ndim)

    def page_spec(a, r):
        def imap(b, g, pt):
            return (layer, pt[b * n_pages + (n_pages - 1 - (g * pp + r))]) + (0,) * (a.ndim - 2)
        return pl.BlockSpec((None, None) + a.shape[2:], imap)

    page_specs, page_args = [], []
    for r in range(pp):
        for a in pools:
            page_specs.append(page_spec(a, r))
            page_args.append(a)
    b_args = (qn, qr, ckvn, krn, qf, fkn, fvn, fcol, frow)
    grid_spec = pltpu.PrefetchScalarGridSpec(
        num_scalar_prefetch=1,
        grid=(nb, ng),
        in_specs=[per_b(a) for a in b_args] + [full(a) for a in consts] + page_specs,
        out_specs=[pl.BlockSpec((None, n_new, H_A * HD), lambda b, g, pt: (b, 0, 0)),
                   pl.BlockSpec((None, n_new, H_F * HD), lambda b, g, pt: (b, 0, 0))],
        scratch_shapes=[pltpu.VMEM((rows, LANES), F32), pltpu.VMEM((rows, LANES), F32),
                        pltpu.VMEM((rows, KV_RANK), F32),
                        pltpu.VMEM((rows, LANES), F32), pltpu.VMEM((rows, LANES), F32),
                        pltpu.VMEM((rows, H_F * HD), F32), pltpu.VMEM((rows, LANES), F32)],
    )
    return pl.pallas_call(
        functools.partial(_sample_attn_kernel, pp=pp, n_new=n_new),
        grid_spec=grid_spec,
        out_shape=[jax.ShapeDtypeStruct((nb, n_new, H_A * HD), F32),
                   jax.ShapeDtypeStruct((nb, n_new, H_F * HD), F32)],
        compiler_params=_params("parallel", "arbitrary"),
        name="sample_attn",
    )(pt.reshape(-1), *b_args, *consts, *page_args)


def _slotted(w, parts):
    out = jnp.zeros(w.shape[:-1] + (H_A * SLOT,), w.dtype)
    for h in range(H_A):
        for src, dst, width in parts:
            s = src(h)
            out = out.at[..., h * SLOT + dst:h * SLOT + dst + width].set(w[..., s:s + width])
    return out


def _slot_vec(nope, rope, pad=0.0):
    one = jnp.concatenate([nope.astype(F32), rope.astype(F32), jnp.full((SLOT - HD - ROPE_DIM,), pad, F32)])
    return jnp.tile(one, H_A)[None, :]


def _group_mean_matrix():
    m = np.zeros((H_A * SLOT, H_A * SLOT), np.float32)
    for h in range(H_A):
        o = h * SLOT
        m[o:o + HD, o:o + HD] = 1.0 / HD
        m[o + HD:o + HD + ROPE_DIM, o + HD:o + HD + ROPE_DIM] = 1.0 / ROPE_DIM
    return jnp.asarray(m, BF)


def _block_mean_matrix(n, g):
    m = np.kron(np.eye(n // g, dtype=np.float32), np.full((g, g), 1.0 / g, np.float32))
    return jnp.asarray(m, BF)


def _rope_tables(pos, dim, lanes_before, lanes_total, ones_before):
    half = dim // 2
    inv = ROPE_BASE ** (-jnp.arange(half, dtype=F32) / half)
    ang = pos.astype(F32)[:, None] * inv[None, :]
    cos, sin = jnp.cos(ang), jnp.sin(ang)
    n = pos.shape[0]
    lead = jnp.ones((n, lanes_before), F32) if ones_before else jnp.zeros((n, lanes_before), F32)
    tail = jnp.zeros((n, lanes_total - lanes_before - dim), F32)
    c = jnp.concatenate([lead, cos, cos, tail], axis=1)
    s = jnp.concatenate([jnp.zeros((n, lanes_before), F32), -sin, sin, tail], axis=1)
    return c, s


def _ret_tables(c):
    lg = jnp.log1p(-jnp.exp2(-5.0 - jnp.arange(H_R, dtype=F32)))
    cq = -(-c // 8) * 8
    i = jnp.arange(cq, dtype=F32)
    j = jnp.arange(QB, dtype=F32)
    diff = i[:, None] - j[None, :]
    ok = (diff >= 0) & (j[None, :] < c) & (i[:, None] < c)
    dec = jnp.where(ok[None], jnp.exp(jnp.maximum(diff, 0.0)[None] * lg[:, None, None]), 0.0)
    qdec = jnp.broadcast_to(jnp.exp((i[None, :] + 1.0) * lg[:, None])[:, :, None], (H_R, cq, DV_R))
    kdec = jnp.exp((c - 1.0 - i)[:, None] * lg[None, :])
    kdec = jnp.where((i < c)[:, None], kdec, 0.0)
    kdec = jnp.repeat(kdec, DK_R, axis=1)
    gc = jnp.broadcast_to(jnp.repeat(jnp.exp(c * lg), DK_R)[:, None], (H_R * DK_R, DV_R))
    return dec.astype(F32), qdec.astype(F32), kdec.astype(F32), gc.astype(F32)


def _prep_weights(l, g_norm, w_in, b_f, g_cq, g_ckv, w_uq, w_uk, w_uv, g_qn, g_qr, g_kn, g_kr,
                  g_fq, g_fk, w_conv, g_ret, w_br, w_out):
    wl = w_in[l]
    cols = []
    for name in _ORDER:
        off, w = _SRC[name]
        wp = _SEG[name][1]
        piece = wl[:, off:off + w]
        if wp != w:
            piece = jnp.pad(piece, ((0, 0), (0, wp - w)))
        cols.append(piece)
    w = {"w_in": jnp.concatenate(cols, axis=1).astype(BF), "g_norm": g_norm[l][None, :]}
    qd = HD + ROPE_DIM
    w["wuq"] = _slotted(w_uq[l], [(lambda h: h * qd, 0, qd)]).astype(BF)
    w["wuk_slot"] = _slotted(w_uk[l], [(lambda h: h * HD, 0, HD)]).astype(BF)
    w["wuk"] = w_uk[l].astype(BF)
    w["wuv"] = w_uv[l].astype(BF)
    w["gcq"] = g_cq[l][None, :]
    w["gckv"] = g_ckv[l][None, :]
    w["gq"] = _slot_vec(g_qn[l], g_qr[l])
    w["gk"] = _slot_vec(g_kn[l], jnp.zeros((ROPE_DIM,), F32))
    w["gk_q"] = _slot_vec(g_kn[l], jnp.ones((ROPE_DIM,), F32))
    w["gkr"] = jnp.pad(g_kr[l], (0, LANES - ROPE_DIM))[None, :]
    w["gfq"] = jnp.tile(g_fq[l], H_F)[None, :] * (HD ** -0.5)
    w["gfk"] = jnp.tile(g_fk[l], H_F)[None, :]
    w["bf"] = jnp.pad(b_f[l], (0, LANES - H_F))[None, :]
    w["wc"] = jnp.pad(w_conv[l], ((0, 8 - CONV_W), (0, 0)))
    w["gret"] = g_ret[l][None, :]
    offs = np.cumsum((0, H_A * HD, H_F * HD, W_C, H_R * DV_R))
    w["wbr"] = tuple(w_br[l][offs[m]:offs[m + 1]].astype(BF) for m in range(4))
    w["wo"] = w_out[l].astype(BF)
    return w


def kernel(x_prompt, x_sample, cache_mla_ckv, cache_mla_krope, cache_fox_k, cache_fox_v, cache_fox_logf,
           state_conv, state_ret, page_table, meta, g_norm, w_in, b_f, g_cq, g_ckv, w_uq, w_uk, w_uv,
           g_mla_qn, g_mla_qr, g_mla_kn, g_mla_kr, g_fox_q, g_fox_k, w_conv, g_ret, w_br, w_out):
    nb, seq, _ = x_prompt.shape
    db, n_new, _ = x_sample.shape
    depth = w_in.shape[0]
    n_pages = page_table.shape[1]
    past = n_pages * PAGE
    pf = (-N_META) % QB
    tp = pf + N_META + seq
    scale_a = (HD + ROPE_DIM) ** -0.5

    mq = _group_mean_matrix()
    m64 = _block_mean_matrix(H_F * HD, HD)
    place = np.zeros((LANES, H_A * SLOT), np.float32)
    for h in range(H_A):
        place[np.arange(ROPE_DIM), h * SLOT + HD + np.arange(ROPE_DIM)] = 1.0
    place = jnp.asarray(place, BF)
    ts = _pick(tp, (640, 384, 128))
    tri = jnp.asarray(np.tril(np.ones((ts, ts), np.float32)), BF)
    ind = jnp.asarray(np.kron(np.eye(H_A, dtype=np.float32), np.full((1, HD), 1.0 / HD, np.float32)), BF)
    upper = jnp.asarray(np.triu(np.ones((PAGE, PAGE), np.float32), 1).T, BF)
    rows = 8 * n_new
    bd = np.zeros((rows, H_A * HD), np.float32)
    for r in range(rows):
        bd[r, (r % 8) * HD:(r % 8 + 1) * HD] = 1.0
    bd = jnp.asarray(bd)

    pos_p = jnp.tile(jnp.arange(tp) - pf, nb)
    pos_s = jnp.tile(past + jnp.arange(n_new), db)
    tabs_p = _rope_tables(pos_p, ROPE_DIM, HD, SLOT, True) + _rope_tables(pos_p, ROPE_DIM, 0, LANES, False)
    tabs_s = _rope_tables(pos_s, ROPE_DIM, HD, SLOT, True) + _rope_tables(pos_s, ROPE_DIM, 0, LANES, False)
    rc_p, rs_p = _rope_tables(jnp.arange(tp) - pf, DK_R, 0, DK_R, False)
    rc_p, rs_p = jnp.tile(rc_p, (1, 2)), jnp.tile(rs_p, (1, 2))
    pos_s8 = past + jnp.arange(8)
    rc_s, rs_s = _rope_tables(pos_s8, DK_R, 0, DK_R, False)
    rc_s, rs_s = jnp.tile(rc_s, (1, 2)), jnp.tile(rs_s, (1, 2))
    ret_p = (rc_p, rs_p) + _ret_tables(QB)
    ret_s = (rc_s, rs_s) + _ret_tables(n_new)

    krt_pool = jnp.swapaxes(cache_mla_krope, 2, 3)
    kt_pool = jnp.transpose(cache_fox_k, (0, 1, 3, 4, 2))
    vt_pool = jnp.transpose(cache_fox_v, (0, 1, 3, 4, 2))
    lft_pool = jnp.swapaxes(cache_fox_logf, 2, 3)
    pools = (cache_mla_ckv, krt_pool, kt_pool, vt_pool, lft_pool)

    h_p = jnp.concatenate([jnp.zeros((nb, pf, D_MODEL), F32),
                           jnp.broadcast_to(meta[None], (nb, N_META, D_MODEL)), x_prompt], axis=1)
    h_p = h_p.reshape(nb * tp, D_MODEL)
    h_s = x_sample.reshape(db * n_new, D_MODEL)
    zeros_state = jnp.zeros((nb, H_R * DK_R, DV_R), F32)
    qmul_p = jnp.full((1, H_A * SLOT), scale_a, F32)

    new_p = [[] for _ in range(7)]
    new_s = [[] for _ in range(7)]
    for l in range(depth):
        w = _prep_weights(l, g_norm, w_in, b_f, g_cq, g_ckv, w_uq, w_uk, w_uv, g_mla_qn, g_mla_qr,
                          g_mla_kn, g_mla_kr, g_fox_q, g_fox_k, w_conv, g_ret, w_br, w_out)
        mla_w = (w["gcq"], w["wuq"], mq, w["gq"], qmul_p, w["gckv"], w["gkr"], w["wuk_slot"], w["gk"], place,
                 w["wuv"])
        fox_w = (w["gfq"], w["gfk"], m64, w["bf"])

        p = _proj(h_p, w["g_norm"], w["w_in"])
        q_a, ckvn, krr, k_a, v_a = _mla_prep(p, tabs_p, mla_w, True)
        fq, fkn, fkb, fvb, logf = _fox_prep(p, fox_w)
        fcum, o_c, conv_st = _seq_prep(p, logf, tri, w["wc"], nb, tp, pf)
        bias = fcum[:, :H_F].reshape(nb, tp, N_PAIR, 2).transpose(0, 2, 3, 1)
        o_a = _flash(q_a, k_a, v_a, None, nb, tp, pf)
        o_b = _flash(fq, fkb, fvb, bias, nb, tp, pf)
        o_d, s_fin = _retention(p, ret_p, w["gret"], zeros_state, nb, tp // QB, QB, pf, None)
        h_p = _merge(h_p, (o_a, o_b, o_c, o_d), p, w["wbr"], w["wo"])
        off_v, w_v = _SEG["f_v"]
        st = (ckvn.reshape(nb, tp, KV_RANK)[:, pf:],
              krr.reshape(nb, tp, LANES)[:, pf:, :ROPE_DIM],
              fkn.reshape(nb, tp, H_F, HD)[:, pf:],
              p[:, off_v:off_v + w_v].reshape(nb, tp, H_F, HD)[:, pf:],
              logf.reshape(nb, tp, LANES)[:, pf:, :H_F],
              conv_st,
              s_fin.reshape(nb, H_R, DK_R, DV_R))
        for j in range(7):
            new_p[j].append(st[j])

        ps = _proj(h_s, w["g_norm"], w["w_in"])
        mla_ws = mla_w[:4] + (scale_a * w["gk_q"],) + mla_w[5:]
        qs, ckvn_s, krr_s = _mla_prep(ps, tabs_s, mla_ws, False)
        fq_s, fkn_s, _, _, logf_s = _fox_prep(ps, fox_w)
        pad8 = lambda a: jnp.pad(a.reshape(db, n_new, a.shape[-1]), ((0, 0), (0, 8 - n_new), (0, 0)))
        ps3 = pad8(ps)
        o_c_s, conv_s, fnew = _sample_local(ps3, pad8(logf_s), state_conv[l], w["wc"], n_new)
        o_d_s, ret_s_new = _retention(ps3, ret_s, w["gret"], state_ret[l].reshape(db, H_R * DK_R, DV_R),
                                      db, 1, 8, 0, None)
        q4 = qs.reshape(db, n_new, H_A, SLOT)
        eye = jnp.eye(H_A, dtype=BF)
        qn_bd = (q4[:, :, :, None, :HD] * eye[None, None, :, :, None]).reshape(db, rows, H_A * HD)
        qr_rep = q4[:, :, :, HD:HD + ROPE_DIM].reshape(db, rows, ROPE_DIM)
        f4 = fq_s.reshape(db, n_new, H_F, HD)
        qf_bd = (f4[:, :, :, None, :] * eye[None, None, :, :, None]).reshape(db, rows, H_F * HD)
        fn = fnew[:, :n_new, :H_F]
        fcol = jnp.broadcast_to(fn.reshape(db, rows, 1), (db, rows, LANES))
        frow = jnp.broadcast_to(jnp.pad(fn.transpose(0, 2, 1), ((0, 0), (0, 0), (0, LANES - n_new)))[:, None],
                                (db, n_new, H_F, LANES)).reshape(db, rows, LANES)
        off_v, w_v = _SEG["f_v"]
        fv_s = ps[:, off_v:off_v + w_v]
        consts = (w["wuk"], w["wuv"], ind, upper, bd)
        o_a_s, o_b_s = _sample_attn(l, page_table, qn_bd, qr_rep, pad8(ckvn_s), pad8(krr_s[:, :ROPE_DIM]),
                                    qf_bd, pad8(fkn_s), pad8(fv_s), fcol, frow, consts, pools, n_new)
        unpad = lambda a: a[:, :n_new].reshape(db * n_new, a.shape[-1])
        h_s = _merge(h_s, (o_a_s.reshape(db * n_new, -1), o_b_s.reshape(db * n_new, -1), unpad(o_c_s), unpad(o_d_s)),
                     ps, w["wbr"], w["wo"])
        st = (ckvn_s.reshape(db, n_new, KV_RANK),
              krr_s[:, :ROPE_DIM].reshape(db, n_new, ROPE_DIM),
              fkn_s.reshape(db, n_new, H_F, HD),
              fv_s.reshape(db, n_new, H_F, HD),
              logf_s[:, :H_F].reshape(db, n_new, H_F),
              conv_s,
              ret_s_new.reshape(db, H_R, DK_R, DV_R))
        for j in range(7):
            new_s[j].append(st[j])

    outs_p = [jnp.stack(a) for a in new_p]
    outs_s = [jnp.stack(a) for a in new_s]
    y_prompt = h_p.reshape(nb, tp, D_MODEL)[:, pf + N_META:]
    y_sample = h_s.reshape(db, n_new, D_MODEL)
    return (y_prompt, y_sample, *outs_p, *outs_s)
```

```python
import functools

import numpy as np
import jax
import jax.numpy as jnp
from jax import lax
from jax.experimental import pallas as pl
from jax.experimental.pallas import tpu as pltpu

D_MODEL = 1024
N_META = 16
HD = 64
H_A = 8
Q_RANK = 256
KV_RANK = 256
ROPE_DIM = 32
H_F = 8
W_C = 512
CONV_W = 3
H_R = 8
DK_R = 64
DV_R = 128
QB = 128
PAGE = 128
ROPE_BASE = 10000.0
EPS = 1e-6
NEG = -1e30

LANES = 128
SLOT = 128
N_PAIR = 4
VMEM_LIMIT = 48 * 1024 * 1024

F32 = jnp.float32
BF = jnp.bfloat16

_ORIG = (("a_cq", 256), ("a_ckv", 256), ("a_kr", 32), ("f_q", 512), ("f_k", 512), ("f_v", 512), ("f_f", 8),
         ("c_u", 512), ("c_b", 512), ("c_c", 512), ("r_q", 512), ("r_k", 512), ("r_v", 1024),
         ("z_a", 512), ("z_b", 512), ("z_c", 512), ("z_d", 1024),
         ("g_a", 1024), ("g_b", 1024), ("g_c", 1024), ("g_d", 1024))
_ORDER = ("g_a", "g_b", "g_c", "g_d", "r_v", "z_d", "f_q", "f_k", "f_v", "c_u", "c_b", "c_c", "r_q", "r_k",
          "z_a", "z_b", "z_c", "a_cq", "a_ckv", "a_kr", "f_f")


def _layout():
    src, off = {}, 0
    for name, w in _ORIG:
        src[name] = (off, w)
        off += w
    seg, off = {}, 0
    for name in _ORDER:
        w = src[name][1]
        wp = -(-w // LANES) * LANES
        assert off % wp == 0
        seg[name] = (off, wp)
        off += wp
    return src, seg, off


_SRC, _SEG, N_P = _layout()


def _cb(name):
    off, w = _SEG[name]
    return off // w


def _pick(n, prefs):
    for t in prefs:
        if n % t == 0:
            return t
    return n


def _dot(a, b):
    return jnp.dot(a, b, preferred_element_type=F32)


def _dot_nt(a, b):
    return lax.dot_general(a, b, (((1,), (1,)), ((), ())), preferred_element_type=F32)


def _dot_tn(a, b):
    return lax.dot_general(a, b, (((0,), (0,)), ((), ())), preferred_element_type=F32)


def _split2_dot(x, m):
    hi = x.astype(BF)
    lo = (x - hi.astype(F32)).astype(BF)
    return _dot(hi, m) + _dot(lo, m)


def _split3(x):
    hi = x.astype(BF)
    r1 = x - hi.astype(F32)
    mid = r1.astype(BF)
    lo = (r1 - mid.astype(F32)).astype(BF)
    return hi, mid, lo


def _rms(x, g, n):
    ms = jnp.sum(x * x, axis=-1, keepdims=True) * (1.0 / n)
    return x * lax.rsqrt(ms + EPS) * g


def _params(*sem):
    return pltpu.CompilerParams(dimension_semantics=sem, vmem_limit_bytes=VMEM_LIMIT)


def _proj_kernel(x_ref, g_ref, w_ref, o_ref, xn_ref):
    @pl.when(pl.program_id(1) == 0)
    def _():
        xn_ref[...] = _rms(x_ref[...], g_ref[...], D_MODEL).astype(BF)

    o_ref[...] = _dot(xn_ref[...], w_ref[...])


def _proj(x, g, w):
    m = x.shape[0]
    tm = _pick(m, (1280, 640, 512, 384, 256, 128))
    tn = _pick(N_P, (896, 128))
    return pl.pallas_call(
        _proj_kernel,
        grid=(m // tm, N_P // tn),
        in_specs=[pl.BlockSpec((tm, D_MODEL), lambda i, j: (i, 0)),
                  pl.BlockSpec((1, D_MODEL), lambda i, j: (0, 0)),
                  pl.BlockSpec((D_MODEL, tn), lambda i, j: (0, j))],
        out_specs=pl.BlockSpec((tm, tn), lambda i, j: (i, j)),
        out_shape=jax.ShapeDtypeStruct((m, N_P), F32),
        scratch_shapes=[pltpu.VMEM((tm, D_MODEL), BF)],
        compiler_params=_params("parallel", "arbitrary"),
        name="proj",
    )(x, g, w)


def _rot_slot(x, cos, sin, first_half_end, half):
    w = x.shape[-1]
    lane = lax.broadcasted_iota(jnp.int32, x.shape, 1) % LANES
    partner = jnp.where(lane < first_half_end, pltpu.roll(x, w - half, 1), pltpu.roll(x, half, 1))
    return x * cos + partner * sin


def _mla_prep_kernel(cq_ref, ckv_ref, kr_ref, cosq_ref, sinq_ref, cosk_ref, sink_ref,
                     gcq_ref, wuq_ref, mq_ref, gq_ref, qmul_ref, gckv_ref, gkr_ref,
                     wuk_ref, gk_ref, place_ref, wuv_ref,
                     q_ref, ckvn_ref, krr_ref, *kv_refs):
    cqn = _rms(cq_ref[...], gcq_ref[...], Q_RANK).astype(BF)
    qraw = _dot(cqn, wuq_ref[...])
    qn = qraw * lax.rsqrt(_split2_dot(qraw * qraw, mq_ref[...]) + EPS) * gq_ref[...]
    cosq = jnp.tile(cosq_ref[...], (1, H_A))
    sinq = jnp.tile(sinq_ref[...], (1, H_A))
    q = _rot_slot(qn, cosq, sinq, HD + ROPE_DIM // 2, ROPE_DIM // 2)
    q_ref[...] = (q * qmul_ref[...]).astype(BF)

    ckvn = _rms(ckv_ref[...], gckv_ref[...], KV_RANK)
    ckvn_ref[...] = ckvn
    krn = _rms(kr_ref[...], gkr_ref[...], ROPE_DIM)
    krr = _rot_slot(krn, cosk_ref[...], sink_ref[...], ROPE_DIM // 2, ROPE_DIM // 2)
    krr_ref[...] = krr

    if kv_refs:
        k_ref, v_ref = kv_refs
        ckvb = ckvn.astype(BF)
        knraw = _dot(ckvb, wuk_ref[...])
        kn = knraw * lax.rsqrt(_split2_dot(knraw * knraw, mq_ref[...]) + EPS) * gk_ref[...]
        k_ref[...] = (kn + _dot(krr.astype(BF), place_ref[...])).astype(BF)
        v_ref[...] = _dot(ckvb, wuv_ref[...]).astype(BF)


def _mla_prep(p, tabs, wts, with_kv):
    m = p.shape[0]
    tm = _pick(m, (256, 128))
    row = lambda w, c: pl.BlockSpec((tm, w), lambda i, c=c: (i, c))
    full = lambda a: pl.BlockSpec(a.shape, lambda i: (0,) * a.ndim)
    out_shape = [jax.ShapeDtypeStruct((m, H_A * SLOT), BF),
                 jax.ShapeDtypeStruct((m, KV_RANK), F32),
                 jax.ShapeDtypeStruct((m, LANES), F32)]
    out_specs = [row(H_A * SLOT, 0), row(KV_RANK, 0), row(LANES, 0)]
    if with_kv:
        out_shape += [jax.ShapeDtypeStruct((m, H_A * SLOT), BF), jax.ShapeDtypeStruct((m, H_A * HD), BF)]
        out_specs += [row(H_A * SLOT, 0), row(H_A * HD, 0)]
    return pl.pallas_call(
        _mla_prep_kernel,
        grid=(m // tm,),
        in_specs=[row(256, _cb("a_cq")), row(256, _cb("a_ckv")), row(LANES, _cb("a_kr"))]
                 + [row(LANES, 0)] * 4 + [full(a) for a in wts],
        out_specs=out_specs,
        out_shape=out_shape,
        compiler_params=_params("parallel"),
        name="mla_prep",
    )(p, p, p, *tabs, *wts)


def _fox_prep_kernel(fq_ref, fk_ref, fv_ref, ff_ref, gq_ref, gk_ref, m64_ref, bf_ref,
                     q_ref, k_ref, kb_ref, vb_ref, logf_ref):
    fq = fq_ref[...]
    fk = fk_ref[...]
    m64 = m64_ref[...]
    q_ref[...] = (fq * lax.rsqrt(_split2_dot(fq * fq, m64) + EPS) * gq_ref[...]).astype(BF)
    kn = fk * lax.rsqrt(_split2_dot(fk * fk, m64) + EPS) * gk_ref[...]
    k_ref[...] = kn
    kb_ref[...] = kn.astype(BF)
    vb_ref[...] = fv_ref[...].astype(BF)
    x = ff_ref[...] + bf_ref[...]
    logf_ref[...] = jnp.minimum(x, 0.0) - jnp.log1p(jnp.exp(-jnp.abs(x)))


def _fox_prep(p, wts):
    m = p.shape[0]
    tm = _pick(m, (256, 128))
    row = lambda w, c: pl.BlockSpec((tm, w), lambda i, c=c: (i, c))
    full = lambda a: pl.BlockSpec(a.shape, lambda i: (0,) * a.ndim)
    w = H_F * HD
    return pl.pallas_call(
        _fox_prep_kernel,
        grid=(m // tm,),
        in_specs=[row(w, _cb("f_q")), row(w, _cb("f_k")), row(w, _cb("f_v")), row(LANES, _cb("f_f"))]
                 + [full(a) for a in wts],
        out_specs=[row(w, 0), row(w, 0), row(w, 0), row(w, 0), row(LANES, 0)],
        out_shape=[jax.ShapeDtypeStruct((m, w), BF), jax.ShapeDtypeStruct((m, w), F32),
                   jax.ShapeDtypeStruct((m, w), BF), jax.ShapeDtypeStruct((m, w), BF),
                   jax.ShapeDtypeStruct((m, LANES), F32)],
        compiler_params=_params("parallel"),
        name="fox_prep",
    )(p, p, p, p, *wts)


def _seq_prep_kernel(logf_ref, cu_ref, cb_ref, cc_ref, tri_ref, wc_ref,
                     fcum_ref, oc_ref, st_ref, carry_ref, ubuf_ref, *, tm, pf):
    t = pl.program_id(1)
    rows = t * tm + lax.broadcasted_iota(jnp.int32, (tm, 1), 0)
    valid = rows >= pf

    @pl.when(t == 0)
    def _():
        carry_ref[...] = jnp.zeros_like(carry_ref)
        ubuf_ref[0:8, :] = jnp.zeros((8, W_C), F32)

    @pl.when(t > 0)
    def _():
        ubuf_ref[0:8, :] = ubuf_ref[tm:tm + 8, :]

    lf = jnp.where(valid, logf_ref[...], 0.0)
    tri = tri_ref[...]
    hi, mid, lo = _split3(lf)
    cs = _dot(tri, hi) + _dot(tri, mid) + _dot(tri, lo) + carry_ref[0:1, :]
    fcum_ref[...] = cs
    carry_ref[...] = jnp.broadcast_to(cs[tm - 1:tm, :], carry_ref.shape)

    u = jnp.where(valid, cc_ref[...] * cu_ref[...], 0.0)
    ubuf_ref[8:8 + tm, :] = u
    wc = wc_ref[...]
    y = wc[0:1, :] * ubuf_ref[6:6 + tm, :] + wc[1:2, :] * ubuf_ref[7:7 + tm, :] + wc[2:3, :] * u
    oc_ref[...] = cb_ref[...] * y
    st_ref[...] = ubuf_ref[tm + 6:tm + 8, :]


def _seq_prep(p, logf, tri, wc, nb, tp, pf):
    tm = tri.shape[0]
    nt = tp // tm
    row = lambda w, c: pl.BlockSpec((tm, w), lambda b, t, c=c: (b * nt + t, c))
    full = lambda a: pl.BlockSpec(a.shape, lambda b, t: (0,) * a.ndim)
    return pl.pallas_call(
        functools.partial(_seq_prep_kernel, tm=tm, pf=pf),
        grid=(nb, nt),
        in_specs=[row(LANES, 0), row(W_C, _cb("c_u")), row(W_C, _cb("c_b")), row(W_C, _cb("c_c")),
                  full(tri), full(wc)],
        out_specs=[row(LANES, 0), row(W_C, 0),
                   pl.BlockSpec((None, CONV_W - 1, W_C), lambda b, t: (b, 0, 0))],
        out_shape=[jax.ShapeDtypeStruct((nb * tp, LANES), F32), jax.ShapeDtypeStruct((nb * tp, W_C), F32),
                   jax.ShapeDtypeStruct((nb, CONV_W - 1, W_C), F32)],
        scratch_shapes=[pltpu.VMEM((8, LANES), F32), pltpu.VMEM((tm + 8, W_C), F32)],
        compiler_params=_params("parallel", "arbitrary"),
        name="seq_prep",
    )(logf, p, p, p, tri, wc)


def _flash_kernel(q_ref, k_ref, v_ref, *rest, t, pf, has_bias):
    if has_bias:
        b_ref, o_ref, m_ref, l_ref, acc_ref = rest
    else:
        o_ref, m_ref, l_ref, acc_ref = rest
    i = pl.program_id(2)
    wq = q_ref.shape[-1]
    reps = t // LANES
    q = q_ref[...]
    lane_q = lax.broadcasted_iota(jnp.int32, (t, wq), 1)
    zero = jnp.zeros_like(q)
    qs = jnp.concatenate([jnp.where(lane_q < wq // 2, q, zero), jnp.where(lane_q >= wq // 2, q, zero)], axis=0)
    first = lax.broadcasted_iota(jnp.int32, (t, LANES), 1) < HD

    m_ref[...] = jnp.full(m_ref.shape, NEG, F32)
    l_ref[...] = jnp.zeros(l_ref.shape, F32)
    acc_ref[...] = jnp.zeros(acc_ref.shape, F32)

    def step(j, masked):
        start = pl.multiple_of(j * t, t)
        k = k_ref[pl.ds(start, t), :]
        v = v_ref[pl.ds(start, t), :]
        s = _dot_nt(qs, k)
        if masked:
            kpos = start + lax.broadcasted_iota(jnp.int32, (t, t), 1)
            qpos = i * t + lax.broadcasted_iota(jnp.int32, (t, t), 0)
            mask = (kpos <= qpos) & (kpos >= pf)
        ps = []
        for hh in range(2):
            rows = slice(hh * t, (hh + 1) * t)
            sh = s[rows]
            if has_bias:
                sh = sh - b_ref[hh:hh + 1, pl.ds(start, t)]
            if masked:
                sh = jnp.where(mask, sh, NEG)
            m_prev = m_ref[rows]
            m_new = jnp.maximum(m_prev, jnp.max(sh, axis=1, keepdims=True))
            alpha = jnp.exp(m_prev - m_new)
            p = jnp.exp(sh - jnp.tile(m_new, (1, reps)))
            l_ref[rows] = alpha * l_ref[rows] + jnp.sum(p, axis=1, keepdims=True)
            m_ref[rows] = m_new
            acc_ref[rows] = acc_ref[rows] * alpha
            ps.append(p.astype(BF))
        acc_ref[...] = acc_ref[...] + _dot(jnp.concatenate(ps, axis=0), v)

    step(0, True)

    def body(j, carry):
        step(j, False)
        return carry

    lax.fori_loop(1, i, body, 0)

    @pl.when(i > 0)
    def _():
        step(i, True)

    o_ref[...] = jnp.where(first, acc_ref[0:t] / l_ref[0:t], acc_ref[t:2 * t] / l_ref[t:2 * t])


def _flash(q, k, v, bias, nb, tp, pf):
    wq = q.shape[1] // N_PAIR
    t = QB * _pick(tp // QB, (5, 3, 1))
    nq = tp // t
    k3 = k.reshape(nb, tp, k.shape[1])
    v3 = v.reshape(nb, tp, v.shape[1])
    in_specs = [pl.BlockSpec((t, wq), lambda b, h, i: (b * nq + i, h)),
                pl.BlockSpec((None, tp, wq), lambda b, h, i: (b, 0, h)),
                pl.BlockSpec((None, tp, LANES), lambda b, h, i: (b, 0, h))]
    args = [q, k3, v3]
    if bias is not None:
        in_specs.append(pl.BlockSpec((None, None, 2, tp), lambda b, h, i: (b, h, 0, 0)))
        args.append(bias)
    return pl.pallas_call(
        functools.partial(_flash_kernel, t=t, pf=pf, has_bias=bias is not None),
        grid=(nb, N_PAIR, nq),
        in_specs=in_specs,
        out_specs=pl.BlockSpec((t, LANES), lambda b, h, i: (b * nq + i, h)),
        out_shape=jax.ShapeDtypeStruct((nb * tp, N_PAIR * LANES), F32),
        scratch_shapes=[pltpu.VMEM((2 * t, LANES), F32)] * 3,
        compiler_params=_params("parallel", "parallel", "arbitrary"),
        name="flash_bias" if bias is not None else "flash",
    )(*args)


def _ret_kernel(rq_ref, rk_ref, rv_ref, cos_ref, sin_ref, dec_ref, qdec_ref, kdec_ref, gc_ref, gret_ref, s0_ref,
                o_ref, sout_ref, s_ref, *, c, pf):
    ci = pl.program_id(1)

    @pl.when(ci == 0)
    def _():
        s_ref[...] = s0_ref[...]

    cos = jnp.tile(cos_ref[...], (1, H_R // 2))
    sin = jnp.tile(sin_ref[...], (1, H_R // 2))
    lane = lax.broadcasted_iota(jnp.int32, (c, H_R * DK_R), 1) % DK_R
    w = H_R * DK_R

    def rot(x):
        partner = jnp.where(lane < DK_R // 2, pltpu.roll(x, w - DK_R // 2, 1), pltpu.roll(x, DK_R // 2, 1))
        return x * cos + partner * sin

    rows = ci * c + lax.broadcasted_iota(jnp.int32, (c, 1), 0)
    valid = rows >= pf
    q = rot(rq_ref[...]).astype(BF)
    k = jnp.where(valid, rot(rk_ref[...]) * (DK_R ** -0.5), 0.0)
    kd = (k * kdec_ref[...]).astype(BF)
    k = k.astype(BF)
    v = jnp.where(valid, rv_ref[...], 0.0).astype(BF)
    if c < QB:
        padrows = lambda x: jnp.concatenate([x, jnp.zeros((QB - c, x.shape[1]), x.dtype)], axis=0)
        k, kd, v = padrows(k), padrows(kd), padrows(v)
    lane_p = lax.broadcasted_iota(jnp.int32, (c, LANES), 1)
    row_p = lax.broadcasted_iota(jnp.int32, (LANES, DV_R), 0)
    zq = jnp.zeros((c, LANES), BF)
    for pp in range(H_R // 2):
        sl = slice(pp * LANES, (pp + 1) * LANES)
        qp, kp, kdp = q[:, sl], k[:, sl], kd[:, sl]
        s_pair = s_ref[sl, :]
        s_b = s_pair.astype(BF)
        upd = []
        for hh in range(2):
            h = 2 * pp + hh
            vh = v[:, h * DV_R:(h + 1) * DV_R]
            qm = jnp.where((lane_p < DK_R) == (hh == 0), qp, zq)
            qk = (_dot_nt(qm, kp) * dec_ref[h]).astype(BF)
            o = _dot(qk, vh) + _dot(qm, s_b) * qdec_ref[h]
            o_ref[:, h * DV_R:(h + 1) * DV_R] = _rms(o, gret_ref[...], DV_R)
            upd.append(_dot_tn(kdp, vh))
        s_ref[sl, :] = gc_ref[sl, :] * s_pair + jnp.where(row_p < DK_R, upd[0], upd[1])
    sout_ref[...] = s_ref[...]


def _retention(p, tabs, gret, s0, nb, nc, c, pf, row_block):
    cos, sin, dec, qdec, kdec, gc = tabs
    if p.ndim == 2:
        row = lambda w, col: pl.BlockSpec((c, w), lambda b, t, col=col: (b * nc + t, col))
        tab = lambda a: pl.BlockSpec((c, a.shape[1]), lambda b, t: (t, 0))
        out_o = pl.BlockSpec((c, H_R * DV_R), lambda b, t: (b * nc + t, 0))
        o_shape = jax.ShapeDtypeStruct((nb * nc * c, H_R * DV_R), F32)
    else:
        row = lambda w, col: pl.BlockSpec((None, c, w), lambda b, t, col=col: (b, 0, col))
        tab = lambda a: pl.BlockSpec((c, a.shape[1]), lambda b, t: (0, 0))
        out_o = pl.BlockSpec((None, c, H_R * DV_R), lambda b, t: (b, 0, 0))
        o_shape = jax.ShapeDtypeStruct((nb, c, H_R * DV_R), F32)
    full = lambda a: pl.BlockSpec(a.shape, lambda b, t: (0,) * a.ndim)
    st = pl.BlockSpec((None, H_R * DK_R, DV_R), lambda b, t: (b, 0, 0))
    return pl.pallas_call(
        functools.partial(_ret_kernel, c=c, pf=pf),
        grid=(nb, nc),
        in_specs=[row(H_R * DK_R, _cb("r_q")), row(H_R * DK_R, _cb("r_k")), row(H_R * DV_R, _cb("r_v")),
                  tab(cos), tab(sin), full(dec), full(qdec), full(kdec), full(gc), full(gret), st],
        out_specs=[out_o, st],
        out_shape=[o_shape, jax.ShapeDtypeStruct((nb, H_R * DK_R, DV_R), F32)],
        scratch_shapes=[pltpu.VMEM((H_R * DK_R, DV_R), F32)],
        compiler_params=_params("parallel", "arbitrary"),
        name="retention",
    )(p, p, p, cos, sin, dec, qdec, kdec, gc, gret, s0)


def _merge_kernel(h_ref, oa_ref, ob_ref, oc_ref, od_ref, za_ref, zb_ref, zc_ref, zd_ref,
                  ga_ref, gb_ref, gc_ref, gd_ref, wa_ref, wb_ref, wc_ref, wd_ref, wo_ref, out_ref):
    acc = None
    for o_ref, z_ref, g_ref, w_ref in ((oa_ref, za_ref, ga_ref, wa_ref), (ob_ref, zb_ref, gb_ref, wb_ref),
                                       (oc_ref, zc_ref, gc_ref, wc_ref), (od_ref, zd_ref, gd_ref, wd_ref)):
        z = z_ref[...]
        x = (o_ref[...] * (z * jax.nn.sigmoid(z))).astype(BF)
        term = jax.nn.sigmoid(g_ref[...]) * _dot(x, w_ref[...])
        acc = term if acc is None else acc + term
    out_ref[...] = h_ref[...] + _dot(acc.astype(BF), wo_ref[...])


def _merge(h, outs, p, wbr, wo):
    m = h.shape[0]
    tm = _pick(m, (256, 128))
    row = lambda w, c: pl.BlockSpec((tm, w), lambda i, c=c: (i, c))
    full = lambda a: pl.BlockSpec(a.shape, lambda i: (0,) * a.ndim)
    widths = (512, 512, 512, 1024)
    return pl.pallas_call(
        _merge_kernel,
        grid=(m // tm,),
        in_specs=[row(D_MODEL, 0)] + [row(w, 0) for w in widths]
                 + [row(w, _cb(n)) for w, n in zip(widths, ("z_a", "z_b", "z_c", "z_d"))]
                 + [row(D_MODEL, _cb(n)) for n in ("g_a", "g_b", "g_c", "g_d")]
                 + [full(a) for a in wbr] + [full(wo)],
        out_specs=row(D_MODEL, 0),
        out_shape=jax.ShapeDtypeStruct((m, D_MODEL), F32),
        compiler_params=_params("parallel"),
        name="merge",
    )(h, *outs, p, p, p, p, p, p, p, p, *wbr, wo)


def _sample_local_kernel(cu_ref, cb_ref, cc_ref, logf_ref, st_ref, wc_ref,
                         oc_ref, stout_ref, fnew_ref, up_ref, *, n_new):
    row = lax.broadcasted_iota(jnp.int32, (8, 1), 0)
    u = cc_ref[...] * cu_ref[...]
    up_ref[0:8, :] = jnp.zeros((8, W_C), F32)
    up_ref[8 - (CONV_W - 1):8, :] = st_ref[...]
    up_ref[8:16, :] = u
    wc = wc_ref[...]
    y = wc[0:1, :] * up_ref[6:14, :] + wc[1:2, :] * up_ref[7:15, :] + wc[2:3, :] * u
    oc_ref[...] = cb_ref[...] * y
    stout_ref[...] = up_ref[8 + n_new - (CONV_W - 1):8 + n_new, :]
    lf = jnp.where(row < n_new, logf_ref[...], 0.0)
    cs = lf
    for j in range(1, n_new):
        cs = cs + jnp.where(row >= j, pltpu.roll(lf, j, 0), 0.0)
    fnew_ref[...] = cs


def _sample_local(p3, logf3, st, wc, n_new):
    nb = p3.shape[0]
    blk = lambda w, c: pl.BlockSpec((None, 8, w), lambda b, c=c: (b, 0, c))
    stb = pl.BlockSpec((None, CONV_W - 1, W_C), lambda b: (b, 0, 0))
    return pl.pallas_call(
        functools.partial(_sample_local_kernel, n_new=n_new),
        grid=(nb,),
        in_specs=[blk(W_C, _cb("c_u")), blk(W_C, _cb("c_b")), blk(W_C, _cb("c_c")), blk(LANES, 0), stb,
                  pl.BlockSpec(wc.shape, lambda b: (0, 0))],
        out_specs=[blk(W_C, 0), stb, blk(LANES, 0)],
        out_shape=[jax.ShapeDtypeStruct((nb, 8, W_C), F32), jax.ShapeDtypeStruct((nb, CONV_W - 1, W_C), F32),
                   jax.ShapeDtypeStruct((nb, 8, LANES), F32)],
        scratch_shapes=[pltpu.VMEM((16, W_C), F32)],
        compiler_params=_params("parallel"),
        name="sample_local",
    )(p3, p3, p3, logf3, st, wc)


def _online(s, m_ref, l_ref, accs):
    m_prev = m_ref[...]
    m_new = jnp.maximum(m_prev, jnp.max(s, axis=1, keepdims=True))
    alpha = jnp.exp(m_prev - m_new)
    p = jnp.exp(s - jnp.tile(m_new, (1, s.shape[1] // LANES)))
    l_ref[...] = alpha * l_ref[...] + jnp.sum(p, axis=1, keepdims=True)
    m_ref[...] = m_new
    pb = p.astype(BF)
    for acc_ref, fn in accs:
        reps = acc_ref.shape[1] // LANES
        acc_ref[...] = acc_ref[...] * jnp.tile(alpha, (1, reps)) + fn(pb)


def _sample_attn_kernel(pt_ref, qn_ref, qr_ref, ckvn_ref, krn_ref, qf_ref, fkn_ref, fvn_ref, fcol_ref, frow_ref,
                        wuk_ref, wuv_ref, ind_ref, u_ref, bd_ref, *rest, pp, n_new):
    pages = rest[:5 * pp]
    oa_ref, ob_ref, ma_ref, la_ref, lat_ref, mf_ref, lf_ref, of_ref, car_ref = rest[5 * pp:]
    g = pl.program_id(1)
    rows = 8 * n_new
    qn = qn_ref[...]
    qr = qr_ref[...]
    qf = qf_ref[...]
    wuk = wuk_ref[...]
    ind = ind_ref[...]

    def mla_scores(ckvb, kr_scores):
        knb = _dot(ckvb, wuk).astype(BF)
        r = lax.rsqrt(_dot_nt(ind, knb * knb) + EPS)
        return _dot_nt(qn, knb) * jnp.tile(r, (n_new, 1)) + kr_scores

    @pl.when(g == 0)
    def _():
        for r in (ma_ref, mf_ref):
            r[...] = jnp.full(r.shape, NEG, F32)
        for r in (la_ref, lat_ref, lf_ref, of_ref):
            r[...] = jnp.zeros(r.shape, F32)
        pad = lambda x: jnp.concatenate([x, jnp.zeros((PAGE - 8, x.shape[1]), x.dtype)], axis=0)
        qrow = lax.broadcasted_iota(jnp.int32, (rows, PAGE), 0) // 8
        kcol = lax.broadcasted_iota(jnp.int32, (rows, PAGE), 1)
        mask = kcol <= qrow
        ckvb = pad(ckvn_ref[...].astype(BF))
        s = mla_scores(ckvb, _dot_nt(qr, pad(krn_ref[...].astype(BF))))
        _online(jnp.where(mask, s, NEG), ma_ref, la_ref, [(lat_ref, lambda pb: _dot(pb, ckvb))])
        fk = pad(fkn_ref[...].astype(BF))
        fv = pad(fvn_ref[...].astype(BF))
        s = _dot_nt(qf, fk) + fcol_ref[...] - frow_ref[...]
        _online(jnp.where(mask, s, NEG), mf_ref, lf_ref, [(of_ref, lambda pb: _dot(pb, fv))])
        car_ref[...] = fcol_ref[...]

    ckvb = jnp.concatenate([pages[5 * r][...].astype(BF) for r in range(pp)], axis=0)
    krt = jnp.concatenate([pages[5 * r + 1][...].astype(BF) for r in range(pp)], axis=1)
    s = mla_scores(ckvb, _dot(qr, krt))
    _online(s, ma_ref, la_ref, [(lat_ref, lambda pb: _dot(pb, ckvb))])

    kt = jnp.concatenate([pages[5 * r + 2][...].reshape(H_F * HD, PAGE).astype(BF) for r in range(pp)], axis=1)
    vt = jnp.concatenate([pages[5 * r + 3][...].reshape(H_F * HD, PAGE).astype(BF) for r in range(pp)], axis=1)
    lft = jnp.concatenate([pages[5 * r + 4][...] for r in range(pp)], axis=0)
    hi, mid, lo = _split3(lft)
    u = u_ref[...]
    both = _dot(hi, u) + _dot(mid, u) + _dot(lo, u)
    car = car_ref[...]
    bias = []
    for r in range(pp):
        bias.append(car + jnp.tile(both[8 * r:8 * r + 8, :PAGE], (n_new, 1)))
        car = car + jnp.tile(both[8 * r:8 * r + 8, PAGE:], (n_new, 1))
    car_ref[...] = car
    s = _dot(qf, kt) + jnp.concatenate(bias, axis=1)
    _online(s, mf_ref, lf_ref, [(of_ref, lambda pb: _dot_nt(pb, vt))])

    @pl.when(g == pl.num_programs(1) - 1)
    def _():
        bd = bd_ref[...]
        lat = (lat_ref[...] / jnp.tile(la_ref[...], (1, KV_RANK // LANES))).astype(BF)
        oa = _dot(lat, wuv_ref[...]) * bd
        oa_ref[...] = jnp.sum(oa.reshape(n_new, 8, H_A * HD), axis=1)
        ob = of_ref[...] / jnp.tile(lf_ref[...], (1, H_F * HD // LANES)) * bd
        ob_ref[...] = jnp.sum(ob.reshape(n_new, 8, H_F * HD), axis=1)


def _sample_attn(layer, pt, qn, qr, ckvn, krn, qf, fkn, fvn, fcol, frow, consts, pools, n_new):
    nb, n_pages = pt.shape
    pp = _pick(n_pages, (8, 4, 2, 1))
    ng = n_pages // pp
    rows = 8 * n_new
    ckv_pool, krt_pool, kt_pool, vt_pool, lft_pool = pools
    per_b = lambda a: pl.BlockSpec((None,) + a.shape[1:], lambda b, g, pt: (b,) + (0,) * (a.ndim - 1))
    full = lambda a: pl.BlockSpec(a.shape, lambda b, g, pt: (0,) * a.ndim)

    def page_spec(a, r):
        def imap(b, g, pt):
            return (layer, pt[b * n_pages + (n_pages - 1 - (g * pp + r))]) + (0,) * (a.ndim - 2)
        return pl.BlockSpec((None, None) + a.shape[2:], imap)

    page_specs, page_args = [], []
    for r in range(pp):
        for a in pools:
            page_specs.append(page_spec(a, r))
            page_args.append(a)
    b_args = (qn, qr, ckvn, krn, qf, fkn, fvn, fcol, frow)
    grid_spec = pltpu.PrefetchScalarGridSpec(
        num_scalar_prefetch=1,
        grid=(nb, ng),
        in_specs=[per_b(a) for a in b_args] + [full(a) for a in consts] + page_specs,
        out_specs=[pl.BlockSpec((None, n_new, H_A * HD), lambda b, g, pt: (b, 0, 0)),
                   pl.BlockSpec((None, n_new, H_F * HD), lambda b, g, pt: (b, 0, 0))],
        scratch_shapes=[pltpu.VMEM((rows, LANES), F32), pltpu.VMEM((rows, LANES), F32),
                        pltpu.VMEM((rows, KV_RANK), F32),
                        pltpu.VMEM((rows, LANES), F32), pltpu.VMEM((rows, LANES), F32),
                        pltpu.VMEM((rows, H_F * HD), F32), pltpu.VMEM((rows, LANES), F32)],
    )
    return pl.pallas_call(
        functools.partial(_sample_attn_kernel, pp=pp, n_new=n_new),
        grid_spec=grid_spec,
        out_shape=[jax.ShapeDtypeStruct((nb, n_new, H_A * HD), F32),
                   jax.ShapeDtypeStruct((nb, n_new, H_F * HD), F32)],
        compiler_params=_params("parallel", "arbitrary"),
        name="sample_attn",
    )(pt.reshape(-1), *b_args, *consts, *page_args)


def _slotted(w, parts):
    out = jnp.zeros(w.shape[:-1] + (H_A * SLOT,), w.dtype)
    for h in range(H_A):
        for src, dst, width in parts:
            s = src(h)
            out = out.at[..., h * SLOT + dst:h * SLOT + dst + width].set(w[..., s:s + width])
    return out


def _slot_vec(nope, rope, pad=0.0):
    one = jnp.concatenate([nope.astype(F32), rope.astype(F32), jnp.full((SLOT - HD - ROPE_DIM,), pad, F32)])
    return jnp.tile(one, H_A)[None, :]


def _group_mean_matrix():
    m = np.zeros((H_A * SLOT, H_A * SLOT), np.float32)
    for h in range(H_A):
        o = h * SLOT
        m[o:o + HD, o:o + HD] = 1.0 / HD
        m[o + HD:o + HD + ROPE_DIM, o + HD:o + HD + ROPE_DIM] = 1.0 / ROPE_DIM
    return jnp.asarray(m, BF)


def _block_mean_matrix(n, g):
    m = np.kron(np.eye(n // g, dtype=np.float32), np.full((g, g), 1.0 / g, np.float32))
    return jnp.asarray(m, BF)


def _rope_tables(pos, dim, lanes_before, lanes_total, ones_before):
    half = dim // 2
    inv = ROPE_BASE ** (-jnp.arange(half, dtype=F32) / half)
    ang = pos.astype(F32)[:, None] * inv[None, :]
    cos, sin = jnp.cos(ang), jnp.sin(ang)
    n = pos.shape[0]
    lead = jnp.ones((n, lanes_before), F32) if ones_before else jnp.zeros((n, lanes_before), F32)
    tail = jnp.zeros((n, lanes_total - lanes_before - dim), F32)
    c = jnp.concatenate([lead, cos, cos, tail], axis=1)
    s = jnp.concatenate([jnp.zeros((n, lanes_before), F32), -sin, sin, tail], axis=1)
    return c, s


def _ret_tables(c):
    lg = jnp.log1p(-jnp.exp2(-5.0 - jnp.arange(H_R, dtype=F32)))
    cq = -(-c // 8) * 8
    i = jnp.arange(cq, dtype=F32)
    j = jnp.arange(QB, dtype=F32)
    diff = i[:, None] - j[None, :]
    ok = (diff >= 0) & (j[None, :] < c) & (i[:, None] < c)
    dec = jnp.where(ok[None], jnp.exp(jnp.maximum(diff, 0.0)[None] * lg[:, None, None]), 0.0)
    qdec = jnp.broadcast_to(jnp.exp((i[None, :] + 1.0) * lg[:, None])[:, :, None], (H_R, cq, DV_R))
    kdec = jnp.exp((c - 1.0 - i)[:, None] * lg[None, :])
    kdec = jnp.where((i < c)[:, None], kdec, 0.0)
    kdec = jnp.repeat(kdec, DK_R, axis=1)
    gc = jnp.broadcast_to(jnp.repeat(jnp.exp(c * lg), DK_R)[:, None], (H_R * DK_R, DV_R))
    return dec.astype(F32), qdec.astype(F32), kdec.astype(F32), gc.astype(F32)


def _prep_weights(l, g_norm, w_in, b_f, g_cq, g_ckv, w_uq, w_uk, w_uv, g_qn, g_qr, g_kn, g_kr,
                  g_fq, g_fk, w_conv, g_ret, w_br, w_out):
    wl = w_in[l]
    cols = []
    for name in _ORDER:
        off, w = _SRC[name]
        wp = _SEG[name][1]
        piece = wl[:, off:off + w]
        if wp != w:
            piece = jnp.pad(piece, ((0, 0), (0, wp - w)))
        cols.append(piece)
    w = {"w_in": jnp.concatenate(cols, axis=1).astype(BF), "g_norm": g_norm[l][None, :]}
    qd = HD + ROPE_DIM
    w["wuq"] = _slotted(w_uq[l], [(lambda h: h * qd, 0, qd)]).astype(BF)
    w["wuk_slot"] = _slotted(w_uk[l], [(lambda h: h * HD, 0, HD)]).astype(BF)
    w["wuk"] = w_uk[l].astype(BF)
    w["wuv"] = w_uv[l].astype(BF)
    w["gcq"] = g_cq[l][None, :]
    w["gckv"] = g_ckv[l][None, :]
    w["gq"] = _slot_vec(g_qn[l], g_qr[l])
    w["gk"] = _slot_vec(g_kn[l], jnp.zeros((ROPE_DIM,), F32))
    w["gk_q"] = _slot_vec(g_kn[l], jnp.ones((ROPE_DIM,), F32))
    w["gkr"] = jnp.pad(g_kr[l], (0, LANES - ROPE_DIM))[None, :]
    w["gfq"] = jnp.tile(g_fq[l], H_F)[None, :] * (HD ** -0.5)
    w["gfk"] = jnp.tile(g_fk[l], H_F)[None, :]
    w["bf"] = jnp.pad(b_f[l], (0, LANES - H_F))[None, :]
    w["wc"] = jnp.pad(w_conv[l], ((0, 8 - CONV_W), (0, 0)))
    w["gret"] = g_ret[l][None, :]
    offs = np.cumsum((0, H_A * HD, H_F * HD, W_C, H_R * DV_R))
    w["wbr"] = tuple(w_br[l][offs[m]:offs[m + 1]].astype(BF) for m in range(4))
    w["wo"] = w_out[l].astype(BF)
    return w


def kernel(x_prompt, x_sample, cache_mla_ckv, cache_mla_krope, cache_fox_k, cache_fox_v, cache_fox_logf,
           state_conv, state_ret, page_table, meta, g_norm, w_in, b_f, g_cq, g_ckv, w_uq, w_uk, w_uv,
           g_mla_qn, g_mla_qr, g_mla_kn, g_mla_kr, g_fox_q, g_fox_k, w_conv, g_ret, w_br, w_out):
    nb, seq, _ = x_prompt.shape
    db, n_new, _ = x_sample.shape
    depth = w_in.shape[0]
    n_pages = page_table.shape[1]
    past = n_pages * PAGE
    pf = (-N_META) % QB
    tp = pf + N_META + seq
    scale_a = (HD + ROPE_DIM) ** -0.5

    mq = _group_mean_matrix()
    m64 = _block_mean_matrix(H_F * HD, HD)
    place = np.zeros((LANES, H_A * SLOT), np.float32)
    for h in range(H_A):
        place[np.arange(ROPE_DIM), h * SLOT + HD + np.arange(ROPE_DIM)] = 1.0
    place = jnp.asarray(place, BF)
    ts = _pick(tp, (640, 384, 128))
    tri = jnp.asarray(np.tril(np.ones((ts, ts), np.float32)), BF)
    ind = jnp.asarray(np.kron(np.eye(H_A, dtype=np.float32), np.full((1, HD), 1.0 / HD, np.float32)), BF)
    upper = jnp.asarray(np.concatenate([np.triu(np.ones((PAGE, PAGE), np.float32), 1).T,
                                        np.ones((PAGE, PAGE), np.float32)], axis=1), BF)
    rows = 8 * n_new
    bd = np.zeros((rows, H_A * HD), np.float32)
    for r in range(rows):
        bd[r, (r % 8) * HD:(r % 8 + 1) * HD] = 1.0
    bd = jnp.asarray(bd)

    pos_p = jnp.tile(jnp.arange(tp) - pf, nb)
    pos_s = jnp.tile(past + jnp.arange(n_new), db)
    tabs_p = _rope_tables(pos_p, ROPE_DIM, HD, SLOT, True) + _rope_tables(pos_p, ROPE_DIM, 0, LANES, False)
    tabs_s = _rope_tables(pos_s, ROPE_DIM, HD, SLOT, True) + _rope_tables(pos_s, ROPE_DIM, 0, LANES, False)
    rc_p, rs_p = _rope_tables(jnp.arange(tp) - pf, DK_R, 0, DK_R, False)
    rc_p, rs_p = jnp.tile(rc_p, (1, 2)), jnp.tile(rs_p, (1, 2))
    pos_s8 = past + jnp.arange(8)
    rc_s, rs_s = _rope_tables(pos_s8, DK_R, 0, DK_R, False)
    rc_s, rs_s = jnp.tile(rc_s, (1, 2)), jnp.tile(rs_s, (1, 2))
    ret_p = (rc_p, rs_p) + _ret_tables(QB)
    ret_s = (rc_s, rs_s) + _ret_tables(n_new)

    krt_pool = jnp.swapaxes(cache_mla_krope, 2, 3)
    kt_pool = jnp.transpose(cache_fox_k, (0, 1, 3, 4, 2))
    vt_pool = jnp.transpose(cache_fox_v, (0, 1, 3, 4, 2))
    lft_pool = jnp.swapaxes(cache_fox_logf, 2, 3)
    pools = (cache_mla_ckv, krt_pool, kt_pool, vt_pool, lft_pool)

    h_p = jnp.concatenate([jnp.zeros((nb, pf, D_MODEL), F32),
                           jnp.broadcast_to(meta[None], (nb, N_META, D_MODEL)), x_prompt], axis=1)
    h_p = h_p.reshape(nb * tp, D_MODEL)
    h_s = x_sample.reshape(db * n_new, D_MODEL)
    zeros_state = jnp.zeros((nb, H_R * DK_R, DV_R), F32)
    qmul_p = jnp.full((1, H_A * SLOT), scale_a, F32)

    new_p = [[] for _ in range(7)]
    new_s = [[] for _ in range(7)]
    for l in range(depth):
        w = _prep_weights(l, g_norm, w_in, b_f, g_cq, g_ckv, w_uq, w_uk, w_uv, g_mla_qn, g_mla_qr,
                          g_mla_kn, g_mla_kr, g_fox_q, g_fox_k, w_conv, g_ret, w_br, w_out)
        mla_w = (w["gcq"], w["wuq"], mq, w["gq"], qmul_p, w["gckv"], w["gkr"], w["wuk_slot"], w["gk"], place,
                 w["wuv"])
        fox_w = (w["gfq"], w["gfk"], m64, w["bf"])

        p = _proj(h_p, w["g_norm"], w["w_in"])
        q_a, ckvn, krr, k_a, v_a = _mla_prep(p, tabs_p, mla_w, True)
        fq, fkn, fkb, fvb, logf = _fox_prep(p, fox_w)
        fcum, o_c, conv_st = _seq_prep(p, logf, tri, w["wc"], nb, tp, pf)
        bias = fcum[:, :H_F].reshape(nb, tp, N_PAIR, 2).transpose(0, 2, 3, 1)
        o_a = _flash(q_a, k_a, v_a, None, nb, tp, pf)
        o_b = _flash(fq, fkb, fvb, bias, nb, tp, pf)
        o_d, s_fin = _retention(p, ret_p, w["gret"], zeros_state, nb, tp // QB, QB, pf, None)
        h_p = _merge(h_p, (o_a, o_b, o_c, o_d), p, w["wbr"], w["wo"])
        off_v, w_v = _SEG["f_v"]
        st = (ckvn.reshape(nb, tp, KV_RANK)[:, pf:],
              krr.reshape(nb, tp, LANES)[:, pf:, :ROPE_DIM],
              fkn.reshape(nb, tp, H_F, HD)[:, pf:],
              p[:, off_v:off_v + w_v].reshape(nb, tp, H_F, HD)[:, pf:],
              logf.reshape(nb, tp, LANES)[:, pf:, :H_F],
              conv_st,
              s_fin.reshape(nb, H_R, DK_R, DV_R))
        for j in range(7):
            new_p[j].append(st[j])

        ps = _proj(h_s, w["g_norm"], w["w_in"])
        mla_ws = mla_w[:4] + (scale_a * w["gk_q"],) + mla_w[5:]
        qs, ckvn_s, krr_s = _mla_prep(ps, tabs_s, mla_ws, False)
        fq_s, fkn_s, _, _, logf_s = _fox_prep(ps, fox_w)
        pad8 = lambda a: jnp.pad(a.reshape(db, n_new, a.shape[-1]), ((0, 0), (0, 8 - n_new), (0, 0)))
        ps3 = pad8(ps)
        o_c_s, conv_s, fnew = _sample_local(ps3, pad8(logf_s), state_conv[l], w["wc"], n_new)
        o_d_s, ret_s_new = _retention(ps3, ret_s, w["gret"], state_ret[l].reshape(db, H_R * DK_R, DV_R),
                                      db, 1, 8, 0, None)
        q4 = qs.reshape(db, n_new, H_A, SLOT)
        eye = jnp.eye(H_A, dtype=BF)
        qn_bd = (q4[:, :, :, None, :HD] * eye[None, None, :, :, None]).reshape(db, rows, H_A * HD)
        qr_rep = q4[:, :, :, HD:HD + ROPE_DIM].reshape(db, rows, ROPE_DIM)
        f4 = fq_s.reshape(db, n_new, H_F, HD)
        qf_bd = (f4[:, :, :, None, :] * eye[None, None, :, :, None]).reshape(db, rows, H_F * HD)
        fn = fnew[:, :n_new, :H_F]
        fcol = jnp.broadcast_to(fn.reshape(db, rows, 1), (db, rows, LANES))
        frow = jnp.broadcast_to(jnp.pad(fn.transpose(0, 2, 1), ((0, 0), (0, 0), (0, LANES - n_new)))[:, None],
                                (db, n_new, H_F, LANES)).reshape(db, rows, LANES)
        off_v, w_v = _SEG["f_v"]
        fv_s = ps[:, off_v:off_v + w_v]
        consts = (w["wuk"], w["wuv"], ind, upper, bd)
        o_a_s, o_b_s = _sample_attn(l, page_table, qn_bd, qr_rep, pad8(ckvn_s), pad8(krr_s[:, :ROPE_DIM]),
                                    qf_bd, pad8(fkn_s), pad8(fv_s), fcol, frow, consts, pools, n_new)
        unpad = lambda a: a[:, :n_new].reshape(db * n_new, a.shape[-1])
        h_s = _merge(h_s, (o_a_s.reshape(db * n_new, -1), o_b_s.reshape(db * n_new, -1), unpad(o_c_s), unpad(o_d_s)),
                     ps, w["wbr"], w["wo"])
        st = (ckvn_s.reshape(db, n_new, KV_RANK),
              krr_s[:, :ROPE_DIM].reshape(db, n_new, ROPE_DIM),
              fkn_s.reshape(db, n_new, H_F, HD),
              fv_s.reshape(db, n_new, H_F, HD),
              logf_s[:, :H_F].reshape(db, n_new, H_F),
              conv_s,
              ret_s_new.reshape(db, H_R, DK_R, DV_R))
        for j in range(7):
            new_s[j].append(st[j])

    outs_p = [jnp.stack(a) for a in new_p]
    outs_s = [jnp.stack(a) for a in new_s]
    y_prompt = h_p.reshape(nb, tp, D_MODEL)[:, pf + N_META:]
    y_sample = h_s.reshape(db, n_new, D_MODEL)
    return (y_prompt, y_sample, *outs_p, *outs_s)
```

```python
import functools

import numpy as np
import jax
import jax.numpy as jnp
from jax import lax
from jax.experimental import pallas as pl
from jax.experimental.pallas import tpu as pltpu

D_MODEL = 1024
N_META = 16
HD = 64
H_A = 8
Q_RANK = 256
KV_RANK = 256
ROPE_DIM = 32
H_F = 8
W_C = 512
CONV_W = 3
H_R = 8
DK_R = 64
DV_R = 128
QB = 128
PAGE = 128
ROPE_BASE = 10000.0
EPS = 1e-6
NEG = -1e30
LOG2E = 1.4426950408889634

LANES = 128
SLOT = 128
N_PAIR = 4
VMEM_LIMIT = 48 * 1024 * 1024

F32 = jnp.float32
BF = jnp.bfloat16

_ORIG = (("a_cq", 256), ("a_ckv", 256), ("a_kr", 32), ("f_q", 512), ("f_k", 512), ("f_v", 512), ("f_f", 8),
         ("c_u", 512), ("c_b", 512), ("c_c", 512), ("r_q", 512), ("r_k", 512), ("r_v", 1024),
         ("z_a", 512), ("z_b", 512), ("z_c", 512), ("z_d", 1024),
         ("g_a", 1024), ("g_b", 1024), ("g_c", 1024), ("g_d", 1024))
_ORDER = ("g_a", "g_b", "g_c", "g_d", "r_v", "z_d", "f_q", "f_k", "f_v", "c_u", "c_b", "c_c", "r_q", "r_k",
          "z_a", "z_b", "z_c", "a_cq", "a_ckv", "a_kr", "f_f")


def _layout():
    src, off = {}, 0
    for name, w in _ORIG:
        src[name] = (off, w)
        off += w
    seg, off = {}, 0
    for name in _ORDER:
        w = src[name][1]
        wp = -(-w // LANES) * LANES
        assert off % wp == 0
        seg[name] = (off, wp)
        off += wp
    return src, seg, off


_SRC, _SEG, N_P = _layout()


def _cb(name):
    off, w = _SEG[name]
    return off // w


def _pick(n, prefs):
    for t in prefs:
        if n % t == 0:
            return t
    return n


def _dot(a, b):
    return jnp.dot(a, b, preferred_element_type=F32)


def _dot_nt(a, b):
    return lax.dot_general(a, b, (((1,), (1,)), ((), ())), preferred_element_type=F32)


def _dot_tn(a, b):
    return lax.dot_general(a, b, (((0,), (0,)), ((), ())), preferred_element_type=F32)


def _split2_dot(x, m):
    hi = x.astype(BF)
    lo = (x - hi.astype(F32)).astype(BF)
    return _dot(hi, m) + _dot(lo, m)


def _split3(x):
    hi = x.astype(BF)
    r1 = x - hi.astype(F32)
    mid = r1.astype(BF)
    lo = (r1 - mid.astype(F32)).astype(BF)
    return hi, mid, lo


def _rms(x, g, n):
    ms = jnp.sum(x * x, axis=-1, keepdims=True) * (1.0 / n)
    return x * lax.rsqrt(ms + EPS) * g


def _params(*sem):
    return pltpu.CompilerParams(dimension_semantics=sem, vmem_limit_bytes=VMEM_LIMIT)


def _proj_kernel(x_ref, g_ref, w_ref, o_ref, xn_ref):
    @pl.when(pl.program_id(1) == 0)
    def _():
        xn_ref[...] = _rms(x_ref[...], g_ref[...], D_MODEL).astype(BF)

    o_ref[...] = _dot(xn_ref[...], w_ref[...])


def _proj(x, g, w):
    m = x.shape[0]
    tm = _pick(m, (1280, 640, 512, 384, 256, 128))
    tn = _pick(N_P, (896, 128))
    return pl.pallas_call(
        _proj_kernel,
        grid=(m // tm, N_P // tn),
        in_specs=[pl.BlockSpec((tm, D_MODEL), lambda i, j: (i, 0)),
                  pl.BlockSpec((1, D_MODEL), lambda i, j: (0, 0)),
                  pl.BlockSpec((D_MODEL, tn), lambda i, j: (0, j))],
        out_specs=pl.BlockSpec((tm, tn), lambda i, j: (i, j)),
        out_shape=jax.ShapeDtypeStruct((m, N_P), F32),
        scratch_shapes=[pltpu.VMEM((tm, D_MODEL), BF)],
        compiler_params=_params("parallel", "arbitrary"),
        name="proj",
    )(x, g, w)


def _rot_slot(x, cos, sin, first_half_end, half):
    w = x.shape[-1]
    lane = lax.broadcasted_iota(jnp.int32, x.shape, 1) % LANES
    partner = jnp.where(lane < first_half_end, pltpu.roll(x, w - half, 1), pltpu.roll(x, half, 1))
    return x * cos + partner * sin


def _mla_prep_kernel(cq_ref, ckv_ref, kr_ref, cosq_ref, sinq_ref, cosk_ref, sink_ref,
                     gcq_ref, wuq_ref, mq_ref, gq_ref, qmul_ref, gckv_ref, gkr_ref,
                     wuk_ref, gk_ref, place_ref, wuv_ref,
                     q_ref, ckvn_ref, krr_ref, *kv_refs):
    cqn = _rms(cq_ref[...], gcq_ref[...], Q_RANK).astype(BF)
    qraw = _dot(cqn, wuq_ref[...])
    qn = qraw * lax.rsqrt(_split2_dot(qraw * qraw, mq_ref[...]) + EPS) * gq_ref[...]
    cosq = jnp.tile(cosq_ref[...], (1, H_A))
    sinq = jnp.tile(sinq_ref[...], (1, H_A))
    q = _rot_slot(qn, cosq, sinq, HD + ROPE_DIM // 2, ROPE_DIM // 2)
    q_ref[...] = (q * qmul_ref[...]).astype(BF)

    ckvn = _rms(ckv_ref[...], gckv_ref[...], KV_RANK)
    ckvn_ref[...] = ckvn
    krn = _rms(kr_ref[...], gkr_ref[...], ROPE_DIM)
    krr = _rot_slot(krn, cosk_ref[...], sink_ref[...], ROPE_DIM // 2, ROPE_DIM // 2)
    krr_ref[...] = krr

    if kv_refs:
        k_ref, v_ref = kv_refs
        ckvb = ckvn.astype(BF)
        knraw = _dot(ckvb, wuk_ref[...])
        kn = knraw * lax.rsqrt(_split2_dot(knraw * knraw, mq_ref[...]) + EPS) * gk_ref[...]
        k_ref[...] = (kn + _dot(krr.astype(BF), place_ref[...])).astype(BF)
        v_ref[...] = _dot(ckvb, wuv_ref[...]).astype(BF)


def _mla_prep(p, tabs, wts, with_kv):
    m = p.shape[0]
    tm = _pick(m, (256, 128))
    row = lambda w, c: pl.BlockSpec((tm, w), lambda i, c=c: (i, c))
    full = lambda a: pl.BlockSpec(a.shape, lambda i: (0,) * a.ndim)
    out_shape = [jax.ShapeDtypeStruct((m, H_A * SLOT), BF),
                 jax.ShapeDtypeStruct((m, KV_RANK), F32),
                 jax.ShapeDtypeStruct((m, LANES), F32)]
    out_specs = [row(H_A * SLOT, 0), row(KV_RANK, 0), row(LANES, 0)]
    if with_kv:
        out_shape += [jax.ShapeDtypeStruct((m, H_A * SLOT), BF), jax.ShapeDtypeStruct((m, H_A * HD), BF)]
        out_specs += [row(H_A * SLOT, 0), row(H_A * HD, 0)]
    return pl.pallas_call(
        _mla_prep_kernel,
        grid=(m // tm,),
        in_specs=[row(256, _cb("a_cq")), row(256, _cb("a_ckv")), row(LANES, _cb("a_kr"))]
                 + [row(LANES, 0)] * 4 + [full(a) for a in wts],
        out_specs=out_specs,
        out_shape=out_shape,
        compiler_params=_params("parallel"),
        name="mla_prep",
    )(p, p, p, *tabs, *wts)


def _fox_prep_kernel(fq_ref, fk_ref, fv_ref, ff_ref, gq_ref, gk_ref, m64_ref, bf_ref,
                     q_ref, k_ref, kb_ref, vb_ref, logf_ref):
    fq = fq_ref[...]
    fk = fk_ref[...]
    m64 = m64_ref[...]
    q_ref[...] = (fq * lax.rsqrt(_split2_dot(fq * fq, m64) + EPS) * gq_ref[...]).astype(BF)
    kn = fk * lax.rsqrt(_split2_dot(fk * fk, m64) + EPS) * gk_ref[...]
    k_ref[...] = kn
    kb_ref[...] = kn.astype(BF)
    vb_ref[...] = fv_ref[...].astype(BF)
    x = ff_ref[...] + bf_ref[...]
    logf_ref[...] = jnp.minimum(x, 0.0) - jnp.log1p(jnp.exp(-jnp.abs(x)))


def _fox_prep(p, wts):
    m = p.shape[0]
    tm = _pick(m, (256, 128))
    row = lambda w, c: pl.BlockSpec((tm, w), lambda i, c=c: (i, c))
    full = lambda a: pl.BlockSpec(a.shape, lambda i: (0,) * a.ndim)
    w = H_F * HD
    return pl.pallas_call(
        _fox_prep_kernel,
        grid=(m // tm,),
        in_specs=[row(w, _cb("f_q")), row(w, _cb("f_k")), row(w, _cb("f_v")), row(LANES, _cb("f_f"))]
                 + [full(a) for a in wts],
        out_specs=[row(w, 0), row(w, 0), row(w, 0), row(w, 0), row(LANES, 0)],
        out_shape=[jax.ShapeDtypeStruct((m, w), BF), jax.ShapeDtypeStruct((m, w), F32),
                   jax.ShapeDtypeStruct((m, w), BF), jax.ShapeDtypeStruct((m, w), BF),
                   jax.ShapeDtypeStruct((m, LANES), F32)],
        compiler_params=_params("parallel"),
        name="fox_prep",
    )(p, p, p, p, *wts)


def _seq_prep_kernel(logf_ref, cu_ref, cb_ref, cc_ref, tri_ref, wc_ref,
                     fcum_ref, oc_ref, st_ref, carry_ref, ubuf_ref, *, tm, pf):
    t = pl.program_id(1)
    rows = t * tm + lax.broadcasted_iota(jnp.int32, (tm, 1), 0)
    valid = rows >= pf

    @pl.when(t == 0)
    def _():
        carry_ref[...] = jnp.zeros_like(carry_ref)
        ubuf_ref[0:8, :] = jnp.zeros((8, W_C), F32)

    @pl.when(t > 0)
    def _():
        ubuf_ref[0:8, :] = ubuf_ref[tm:tm + 8, :]

    lf = jnp.where(valid, logf_ref[...], 0.0)
    tri = tri_ref[...]
    hi, mid, lo = _split3(lf)
    cs = _dot(tri, hi) + _dot(tri, mid) + _dot(tri, lo) + carry_ref[0:1, :]
    fcum_ref[...] = cs * LOG2E
    carry_ref[...] = jnp.broadcast_to(cs[tm - 1:tm, :], carry_ref.shape)

    u = jnp.where(valid, cc_ref[...] * cu_ref[...], 0.0)
    ubuf_ref[8:8 + tm, :] = u
    wc = wc_ref[...]
    y = wc[0:1, :] * ubuf_ref[6:6 + tm, :] + wc[1:2, :] * ubuf_ref[7:7 + tm, :] + wc[2:3, :] * u
    oc_ref[...] = cb_ref[...] * y
    st_ref[...] = ubuf_ref[tm + 6:tm + 8, :]


def _seq_prep(p, logf, tri, wc, nb, tp, pf):
    tm = tri.shape[0]
    nt = tp // tm
    row = lambda w, c: pl.BlockSpec((tm, w), lambda b, t, c=c: (b * nt + t, c))
    full = lambda a: pl.BlockSpec(a.shape, lambda b, t: (0,) * a.ndim)
    return pl.pallas_call(
        functools.partial(_seq_prep_kernel, tm=tm, pf=pf),
        grid=(nb, nt),
        in_specs=[row(LANES, 0), row(W_C, _cb("c_u")), row(W_C, _cb("c_b")), row(W_C, _cb("c_c")),
                  full(tri), full(wc)],
        out_specs=[row(LANES, 0), row(W_C, 0),
                   pl.BlockSpec((None, CONV_W - 1, W_C), lambda b, t: (b, 0, 0))],
        out_shape=[jax.ShapeDtypeStruct((nb * tp, LANES), F32), jax.ShapeDtypeStruct((nb * tp, W_C), F32),
                   jax.ShapeDtypeStruct((nb, CONV_W - 1, W_C), F32)],
        scratch_shapes=[pltpu.VMEM((8, LANES), F32), pltpu.VMEM((tm + 8, W_C), F32)],
        compiler_params=_params("parallel", "arbitrary"),
        name="seq_prep",
    )(logf, p, p, p, tri, wc)


def _flash_kernel(q_ref, k_ref, v_ref, *rest, t, pf, has_bias):
    if has_bias:
        b_ref, o_ref, m_ref, l_ref, acc_ref = rest
    else:
        o_ref, m_ref, l_ref, acc_ref = rest
    i = pl.program_id(2)
    wq = q_ref.shape[-1]
    reps = t // LANES
    q = q_ref[...]
    lane_q = lax.broadcasted_iota(jnp.int32, (t, wq), 1)
    zero = jnp.zeros_like(q)
    qs = jnp.concatenate([jnp.where(lane_q < wq // 2, q, zero), jnp.where(lane_q >= wq // 2, q, zero)], axis=0)
    first = lax.broadcasted_iota(jnp.int32, (t, LANES), 1) < HD

    m_ref[...] = jnp.full(m_ref.shape, NEG, F32)
    l_ref[...] = jnp.zeros(l_ref.shape, F32)
    acc_ref[...] = jnp.zeros(acc_ref.shape, F32)

    def step(j, masked):
        start = pl.multiple_of(j * t, t)
        k = k_ref[pl.ds(start, t), :]
        v = v_ref[pl.ds(start, t), :]
        s = _dot_nt(qs, k)
        if masked:
            kpos = start + lax.broadcasted_iota(jnp.int32, (t, t), 1)
            qpos = i * t + lax.broadcasted_iota(jnp.int32, (t, t), 0)
            mask = (kpos <= qpos) & (kpos >= pf)
        ps = []
        for hh in range(2):
            rows = slice(hh * t, (hh + 1) * t)
            sh = s[rows]
            if has_bias:
                sh = sh - b_ref[hh:hh + 1, pl.ds(start, t)]
            if masked:
                sh = jnp.where(mask, sh, NEG)
            m_prev = m_ref[rows]
            m_new = jnp.maximum(m_prev, jnp.max(sh, axis=1, keepdims=True))
            alpha = jnp.exp2(m_prev - m_new)
            p = jnp.exp2(sh - jnp.tile(m_new, (1, reps)))
            l_ref[rows] = alpha * l_ref[rows] + jnp.sum(p, axis=1, keepdims=True)
            m_ref[rows] = m_new
            acc_ref[rows] = acc_ref[rows] * alpha
            ps.append(p.astype(BF))
        acc_ref[...] = acc_ref[...] + _dot(jnp.concatenate(ps, axis=0), v)

    step(0, True)

    def body(j, carry):
        step(j, False)
        return carry

    lax.fori_loop(1, i, body, 0)

    @pl.when(i > 0)
    def _():
        step(i, True)

    o_ref[...] = jnp.where(first, acc_ref[0:t] / l_ref[0:t], acc_ref[t:2 * t] / l_ref[t:2 * t])


def _flash(q, k, v, bias, nb, tp, pf):
    wq = q.shape[1] // N_PAIR
    t = QB * _pick(tp // QB, (5, 3, 1))
    nq = tp // t
    k3 = k.reshape(nb, tp, k.shape[1])
    v3 = v.reshape(nb, tp, v.shape[1])
    in_specs = [pl.BlockSpec((t, wq), lambda b, h, i: (b * nq + i, h)),
                pl.BlockSpec((None, tp, wq), lambda b, h, i: (b, 0, h)),
                pl.BlockSpec((None, tp, LANES), lambda b, h, i: (b, 0, h))]
    args = [q, k3, v3]
    if bias is not None:
        in_specs.append(pl.BlockSpec((None, None, 2, tp), lambda b, h, i: (b, h, 0, 0)))
        args.append(bias)
    return pl.pallas_call(
        functools.partial(_flash_kernel, t=t, pf=pf, has_bias=bias is not None),
        grid=(nb, N_PAIR, nq),
        in_specs=in_specs,
        out_specs=pl.BlockSpec((t, LANES), lambda b, h, i: (b * nq + i, h)),
        out_shape=jax.ShapeDtypeStruct((nb * tp, N_PAIR * LANES), F32),
        scratch_shapes=[pltpu.VMEM((2 * t, LANES), F32)] * 3,
        compiler_params=_params("parallel", "parallel", "arbitrary"),
        name="flash_bias" if bias is not None else "flash",
    )(*args)


def _ret_kernel(rq_ref, rk_ref, rv_ref, cos_ref, sin_ref, dec_ref, qdec_ref, kdec_ref, gc_ref, gret_ref, s0_ref,
                o_ref, sout_ref, s_ref, *, c, pf):
    ci = pl.program_id(1)

    @pl.when(ci == 0)
    def _():
        s_ref[...] = s0_ref[...]

    cos = jnp.tile(cos_ref[...], (1, H_R // 2))
    sin = jnp.tile(sin_ref[...], (1, H_R // 2))
    lane = lax.broadcasted_iota(jnp.int32, (c, H_R * DK_R), 1) % DK_R
    w = H_R * DK_R

    def rot(x):
        partner = jnp.where(lane < DK_R // 2, pltpu.roll(x, w - DK_R // 2, 1), pltpu.roll(x, DK_R // 2, 1))
        return x * cos + partner * sin

    rows = ci * c + lax.broadcasted_iota(jnp.int32, (c, 1), 0)
    valid = rows >= pf
    q = rot(rq_ref[...]).astype(BF)
    k = jnp.where(valid, rot(rk_ref[...]) * (DK_R ** -0.5), 0.0)
    kd = (k * kdec_ref[...]).astype(BF)
    k = k.astype(BF)
    v = jnp.where(valid, rv_ref[...], 0.0).astype(BF)
    if c < QB:
        padrows = lambda x: jnp.concatenate([x, jnp.zeros((QB - c, x.shape[1]), x.dtype)], axis=0)
        k, kd, v = padrows(k), padrows(kd), padrows(v)
    lane_p = lax.broadcasted_iota(jnp.int32, (c, LANES), 1)
    row_p = lax.broadcasted_iota(jnp.int32, (LANES, DV_R), 0)
    zq = jnp.zeros((c, LANES), BF)
    for pp in range(H_R // 2):
        sl = slice(pp * LANES, (pp + 1) * LANES)
        qp, kp, kdp = q[:, sl], k[:, sl], kd[:, sl]
        s_pair = s_ref[sl, :]
        s_b = s_pair.astype(BF)
        upd = []
        for hh in range(2):
            h = 2 * pp + hh
            vh = v[:, h * DV_R:(h + 1) * DV_R]
            qm = jnp.where((lane_p < DK_R) == (hh == 0), qp, zq)
            qk = (_dot_nt(qm, kp) * dec_ref[h]).astype(BF)
            o = _dot(qk, vh) + _dot(qm, s_b) * qdec_ref[h]
            o_ref[:, h * DV_R:(h + 1) * DV_R] = _rms(o, gret_ref[...], DV_R)
            upd.append(_dot_tn(kdp, vh))
        s_ref[sl, :] = gc_ref[sl, :] * s_pair + jnp.where(row_p < DK_R, upd[0], upd[1])
    sout_ref[...] = s_ref[...]


def _retention(p, tabs, gret, s0, nb, nc, c, pf, row_block):
    cos, sin, dec, qdec, kdec, gc = tabs
    if p.ndim == 2:
        row = lambda w, col: pl.BlockSpec((c, w), lambda b, t, col=col: (b * nc + t, col))
        tab = lambda a: pl.BlockSpec((c, a.shape[1]), lambda b, t: (t, 0))
        out_o = pl.BlockSpec((c, H_R * DV_R), lambda b, t: (b * nc + t, 0))
        o_shape = jax.ShapeDtypeStruct((nb * nc * c, H_R * DV_R), F32)
    else:
        row = lambda w, col: pl.BlockSpec((None, c, w), lambda b, t, col=col: (b, 0, col))
        tab = lambda a: pl.BlockSpec((c, a.shape[1]), lambda b, t: (0, 0))
        out_o = pl.BlockSpec((None, c, H_R * DV_R), lambda b, t: (b, 0, 0))
        o_shape = jax.ShapeDtypeStruct((nb, c, H_R * DV_R), F32)
    full = lambda a: pl.BlockSpec(a.shape, lambda b, t: (0,) * a.ndim)
    st = pl.BlockSpec((None, H_R * DK_R, DV_R), lambda b, t: (b, 0, 0))
    return pl.pallas_call(
        functools.partial(_ret_kernel, c=c, pf=pf),
        grid=(nb, nc),
        in_specs=[row(H_R * DK_R, _cb("r_q")), row(H_R * DK_R, _cb("r_k")), row(H_R * DV_R, _cb("r_v")),
                  tab(cos), tab(sin), full(dec), full(qdec), full(kdec), full(gc), full(gret), st],
        out_specs=[out_o, st],
        out_shape=[o_shape, jax.ShapeDtypeStruct((nb, H_R * DK_R, DV_R), F32)],
        scratch_shapes=[pltpu.VMEM((H_R * DK_R, DV_R), F32)],
        compiler_params=_params("parallel", "arbitrary"),
        name="retention",
    )(p, p, p, cos, sin, dec, qdec, kdec, gc, gret, s0)


def _merge_kernel(h_ref, oa_ref, ob_ref, oc_ref, od_ref, za_ref, zb_ref, zc_ref, zd_ref,
                  ga_ref, gb_ref, gc_ref, gd_ref, wa_ref, wb_ref, wc_ref, wd_ref, wo_ref, out_ref):
    acc = None
    for o_ref, z_ref, g_ref, w_ref in ((oa_ref, za_ref, ga_ref, wa_ref), (ob_ref, zb_ref, gb_ref, wb_ref),
                                       (oc_ref, zc_ref, gc_ref, wc_ref), (od_ref, zd_ref, gd_ref, wd_ref)):
        z = z_ref[...]
        x = (o_ref[...] * (z * jax.nn.sigmoid(z))).astype(BF)
        term = jax.nn.sigmoid(g_ref[...]) * _dot(x, w_ref[...])
        acc = term if acc is None else acc + term
    out_ref[...] = h_ref[...] + _dot(acc.astype(BF), wo_ref[...])


def _merge(h, outs, p, wbr, wo):
    m = h.shape[0]
    tm = _pick(m, (256, 128))
    row = lambda w, c: pl.BlockSpec((tm, w), lambda i, c=c: (i, c))
    full = lambda a: pl.BlockSpec(a.shape, lambda i: (0,) * a.ndim)
    widths = (512, 512, 512, 1024)
    return pl.pallas_call(
        _merge_kernel,
        grid=(m // tm,),
        in_specs=[row(D_MODEL, 0)] + [row(w, 0) for w in widths]
                 + [row(w, _cb(n)) for w, n in zip(widths, ("z_a", "z_b", "z_c", "z_d"))]
                 + [row(D_MODEL, _cb(n)) for n in ("g_a", "g_b", "g_c", "g_d")]
                 + [full(a) for a in wbr] + [full(wo)],
        out_specs=row(D_MODEL, 0),
        out_shape=jax.ShapeDtypeStruct((m, D_MODEL), F32),
        compiler_params=_params("parallel"),
        name="merge",
    )(h, *outs, p, p, p, p, p, p, p, p, *wbr, wo)


def _sample_local_kernel(cu_ref, cb_ref, cc_ref, logf_ref, st_ref, wc_ref,
                         oc_ref, stout_ref, fnew_ref, up_ref, *, n_new):
    row = lax.broadcasted_iota(jnp.int32, (8, 1), 0)
    u = cc_ref[...] * cu_ref[...]
    up_ref[0:8, :] = jnp.zeros((8, W_C), F32)
    up_ref[8 - (CONV_W - 1):8, :] = st_ref[...]
    up_ref[8:16, :] = u
    wc = wc_ref[...]
    y = wc[0:1, :] * up_ref[6:14, :] + wc[1:2, :] * up_ref[7:15, :] + wc[2:3, :] * u
    oc_ref[...] = cb_ref[...] * y
    stout_ref[...] = up_ref[8 + n_new - (CONV_W - 1):8 + n_new, :]
    lf = jnp.where(row < n_new, logf_ref[...], 0.0)
    cs = lf
    for j in range(1, n_new):
        cs = cs + jnp.where(row >= j, pltpu.roll(lf, j, 0), 0.0)
    fnew_ref[...] = cs


def _sample_local(p3, logf3, st, wc, n_new):
    nb = p3.shape[0]
    blk = lambda w, c: pl.BlockSpec((None, 8, w), lambda b, c=c: (b, 0, c))
    stb = pl.BlockSpec((None, CONV_W - 1, W_C), lambda b: (b, 0, 0))
    return pl.pallas_call(
        functools.partial(_sample_local_kernel, n_new=n_new),
        grid=(nb,),
        in_specs=[blk(W_C, _cb("c_u")), blk(W_C, _cb("c_b")), blk(W_C, _cb("c_c")), blk(LANES, 0), stb,
                  pl.BlockSpec(wc.shape, lambda b: (0, 0))],
        out_specs=[blk(W_C, 0), stb, blk(LANES, 0)],
        out_shape=[jax.ShapeDtypeStruct((nb, 8, W_C), F32), jax.ShapeDtypeStruct((nb, CONV_W - 1, W_C), F32),
                   jax.ShapeDtypeStruct((nb, 8, LANES), F32)],
        scratch_shapes=[pltpu.VMEM((16, W_C), F32)],
        compiler_params=_params("parallel"),
        name="sample_local",
    )(p3, p3, p3, logf3, st, wc)


def _online(s, m_ref, l_ref, accs):
    m_prev = m_ref[...]
    m_new = jnp.maximum(m_prev, jnp.max(s, axis=1, keepdims=True))
    alpha = jnp.exp(m_prev - m_new)
    p = jnp.exp(s - jnp.tile(m_new, (1, s.shape[1] // LANES)))
    l_ref[...] = alpha * l_ref[...] + jnp.sum(p, axis=1, keepdims=True)
    m_ref[...] = m_new
    pb = p.astype(BF)
    for acc_ref, fn in accs:
        reps = acc_ref.shape[1] // LANES
        acc_ref[...] = acc_ref[...] * jnp.tile(alpha, (1, reps)) + fn(pb)


def _sample_attn_kernel(pt_ref, qn_ref, qr_ref, ckvn_ref, krn_ref, qf_ref, fkn_ref, fvn_ref, fcol_ref, frow_ref,
                        wuk_ref, wuv_ref, ind_ref, u_ref, bd_ref, *rest, pp, n_new):
    pages = rest[:5 * pp]
    oa_ref, ob_ref, ma_ref, la_ref, lat_ref, mf_ref, lf_ref, of_ref, car_ref = rest[5 * pp:]
    g = pl.program_id(1)
    rows = 8 * n_new
    qn = qn_ref[...]
    qr = qr_ref[...]
    qf = qf_ref[...]
    wuk = wuk_ref[...]
    ind = ind_ref[...]

    qabs = _dot_nt(qn, wuk).astype(BF)

    def mla_scores(ckvb, kr_scores):
        knb = _dot(ckvb, wuk).astype(BF)
        r = lax.rsqrt(_dot_nt(ind, knb * knb) + EPS)
        return _dot_nt(qabs, ckvb) * jnp.tile(r, (n_new, 1)) + kr_scores

    @pl.when(g == 0)
    def _():
        for r in (ma_ref, mf_ref):
            r[...] = jnp.full(r.shape, NEG, F32)
        for r in (la_ref, lat_ref, lf_ref, of_ref):
            r[...] = jnp.zeros(r.shape, F32)
        pad = lambda x: jnp.concatenate([x, jnp.zeros((PAGE - 8, x.shape[1]), x.dtype)], axis=0)
        qrow = lax.broadcasted_iota(jnp.int32, (rows, PAGE), 0) // 8
        kcol = lax.broadcasted_iota(jnp.int32, (rows, PAGE), 1)
        mask = kcol <= qrow
        ckvb = pad(ckvn_ref[...].astype(BF))
        s = mla_scores(ckvb, _dot_nt(qr, pad(krn_ref[...].astype(BF))))
        _online(jnp.where(mask, s, NEG), ma_ref, la_ref, [(lat_ref, lambda pb: _dot(pb, ckvb))])
        fk = pad(fkn_ref[...].astype(BF))
        fv = pad(fvn_ref[...].astype(BF))
        s = _dot_nt(qf, fk) + fcol_ref[...] - frow_ref[...]
        _online(jnp.where(mask, s, NEG), mf_ref, lf_ref, [(of_ref, lambda pb: _dot(pb, fv))])
        car_ref[...] = fcol_ref[...]

    ckvb = jnp.concatenate([pages[5 * r][...].astype(BF) for r in range(pp)], axis=0)
    krt = jnp.concatenate([pages[5 * r + 1][...].astype(BF) for r in range(pp)], axis=1)
    s = mla_scores(ckvb, _dot(qr, krt))
    _online(s, ma_ref, la_ref, [(lat_ref, lambda pb: _dot(pb, ckvb))])

    kt = jnp.concatenate([pages[5 * r + 2][...].reshape(H_F * HD, PAGE).astype(BF) for r in range(pp)], axis=1)
    vt = jnp.concatenate([pages[5 * r + 3][...].reshape(H_F * HD, PAGE).astype(BF) for r in range(pp)], axis=1)
    lft = jnp.concatenate([pages[5 * r + 4][...] for r in range(pp)], axis=0)
    hi, mid, lo = _split3(lft)
    u = u_ref[...]
    both = _dot(hi, u) + _dot(mid, u) + _dot(lo, u)
    car = car_ref[...]
    bias = []
    for r in range(pp):
        bias.append(car + jnp.tile(both[8 * r:8 * r + 8, :PAGE], (n_new, 1)))
        car = car + jnp.tile(both[8 * r:8 * r + 8, PAGE:], (n_new, 1))
    car_ref[...] = car
    s = _dot(qf, kt) + jnp.concatenate(bias, axis=1)
    _online(s, mf_ref, lf_ref, [(of_ref, lambda pb: _dot_nt(pb, vt))])

    @pl.when(g == pl.num_programs(1) - 1)
    def _():
        bd = bd_ref[...]
        lat = (lat_ref[...] / jnp.tile(la_ref[...], (1, KV_RANK // LANES))).astype(BF)
        oa = _dot(lat, wuv_ref[...]) * bd
        oa_ref[...] = jnp.sum(oa.reshape(n_new, 8, H_A * HD), axis=1)
        ob = of_ref[...] / jnp.tile(lf_ref[...], (1, H_F * HD // LANES)) * bd
        ob_ref[...] = jnp.sum(ob.reshape(n_new, 8, H_F * HD), axis=1)


def _sample_attn(layer, pt, qn, qr, ckvn, krn, qf, fkn, fvn, fcol, frow, consts, pools, n_new):
    nb, n_pages = pt.shape
    pp = _pick(n_pages, (16, 8, 4, 2, 1))
    ng = n_pages // pp
    rows = 8 * n_new
    ckv_pool, krt_pool, kt_pool, vt_pool, lft_pool = pools
    per_b = lambda a: pl.BlockSpec((None,) + a.shape[1:], lambda b, g, pt: (b,) + (0,) * (a.ndim - 1))
    full = lambda a: pl.BlockSpec(a.shape, lambda b, g, pt: (0,) * a.ndim)

    def page_spec(a, r):
        def imap(b, g, pt):
            return (layer, pt[b * n_pages + (n_pages - 1 - (g * pp + r))]) + (0,) * (a.ndim - 2)
        return pl.BlockSpec((None, None) + a.shape[2:], imap)

    page_specs, page_args = [], []
    for r in range(pp):
        for a in pools:
            page_specs.append(page_spec(a, r))
            page_args.append(a)
    b_args = (qn, qr, ckvn, krn, qf, fkn, fvn, fcol, frow)
    grid_spec = pltpu.PrefetchScalarGridSpec(
        num_scalar_prefetch=1,
        grid=(nb, ng),
        in_specs=[per_b(a) for a in b_args] + [full(a) for a in consts] + page_specs,
        out_specs=[pl.BlockSpec((None, n_new, H_A * HD), lambda b, g, pt: (b, 0, 0)),
                   pl.BlockSpec((None, n_new, H_F * HD), lambda b, g, pt: (b, 0, 0))],
        scratch_shapes=[pltpu.VMEM((rows, LANES), F32), pltpu.VMEM((rows, LANES), F32),
                        pltpu.VMEM((rows, KV_RANK), F32),
                        pltpu.VMEM((rows, LANES), F32), pltpu.VMEM((rows, LANES), F32),
                        pltpu.VMEM((rows, H_F * HD), F32), pltpu.VMEM((rows, LANES), F32)],
    )
    return pl.pallas_call(
        functools.partial(_sample_attn_kernel, pp=pp, n_new=n_new),
        grid_spec=grid_spec,
        out_shape=[jax.ShapeDtypeStruct((nb, n_new, H_A * HD), F32),
                   jax.ShapeDtypeStruct((nb, n_new, H_F * HD), F32)],
        compiler_params=_params("parallel", "arbitrary"),
        name="sample_attn",
    )(pt.reshape(-1), *b_args, *consts, *page_args)


def _slotted(w, parts):
    out = jnp.zeros(w.shape[:-1] + (H_A * SLOT,), w.dtype)
    for h in range(H_A):
        for src, dst, width in parts:
            s = src(h)
            out = out.at[..., h * SLOT + dst:h * SLOT + dst + width].set(w[..., s:s + width])
    return out


def _slot_vec(nope, rope, pad=0.0):
    one = jnp.concatenate([nope.astype(F32), rope.astype(F32), jnp.full((SLOT - HD - ROPE_DIM,), pad, F32)])
    return jnp.tile(one, H_A)[None, :]


def _group_mean_matrix():
    m = np.zeros((H_A * SLOT, H_A * SLOT), np.float32)
    for h in range(H_A):
        o = h * SLOT
        m[o:o + HD, o:o + HD] = 1.0 / HD
        m[o + HD:o + HD + ROPE_DIM, o + HD:o + HD + ROPE_DIM] = 1.0 / ROPE_DIM
    return jnp.asarray(m, BF)


def _block_mean_matrix(n, g):
    m = np.kron(np.eye(n // g, dtype=np.float32), np.full((g, g), 1.0 / g, np.float32))
    return jnp.asarray(m, BF)


def _rope_tables(pos, dim, lanes_before, lanes_total, ones_before):
    half = dim // 2
    inv = ROPE_BASE ** (-jnp.arange(half, dtype=F32) / half)
    ang = pos.astype(F32)[:, None] * inv[None, :]
    cos, sin = jnp.cos(ang), jnp.sin(ang)
    n = pos.shape[0]
    lead = jnp.ones((n, lanes_before), F32) if ones_before else jnp.zeros((n, lanes_before), F32)
    tail = jnp.zeros((n, lanes_total - lanes_before - dim), F32)
    c = jnp.concatenate([lead, cos, cos, tail], axis=1)
    s = jnp.concatenate([jnp.zeros((n, lanes_before), F32), -sin, sin, tail], axis=1)
    return c, s


def _ret_tables(c):
    lg = jnp.log1p(-jnp.exp2(-5.0 - jnp.arange(H_R, dtype=F32)))
    cq = -(-c // 8) * 8
    i = jnp.arange(cq, dtype=F32)
    j = jnp.arange(QB, dtype=F32)
    diff = i[:, None] - j[None, :]
    ok = (diff >= 0) & (j[None, :] < c) & (i[:, None] < c)
    dec = jnp.where(ok[None], jnp.exp(jnp.maximum(diff, 0.0)[None] * lg[:, None, None]), 0.0)
    qdec = jnp.broadcast_to(jnp.exp((i[None, :] + 1.0) * lg[:, None])[:, :, None], (H_R, cq, DV_R))
    kdec = jnp.exp((c - 1.0 - i)[:, None] * lg[None, :])
    kdec = jnp.where((i < c)[:, None], kdec, 0.0)
    kdec = jnp.repeat(kdec, DK_R, axis=1)
    gc = jnp.broadcast_to(jnp.repeat(jnp.exp(c * lg), DK_R)[:, None], (H_R * DK_R, DV_R))
    return dec.astype(F32), qdec.astype(F32), kdec.astype(F32), gc.astype(F32)


def _prep_weights(l, g_norm, w_in, b_f, g_cq, g_ckv, w_uq, w_uk, w_uv, g_qn, g_qr, g_kn, g_kr,
                  g_fq, g_fk, w_conv, g_ret, w_br, w_out):
    wl = w_in[l]
    cols = []
    for name in _ORDER:
        off, w = _SRC[name]
        wp = _SEG[name][1]
        piece = wl[:, off:off + w]
        if wp != w:
            piece = jnp.pad(piece, ((0, 0), (0, wp - w)))
        cols.append(piece)
    w = {"w_in": jnp.concatenate(cols, axis=1).astype(BF), "g_norm": g_norm[l][None, :]}
    qd = HD + ROPE_DIM
    w["wuq"] = _slotted(w_uq[l], [(lambda h: h * qd, 0, qd)]).astype(BF)
    w["wuk_slot"] = _slotted(w_uk[l], [(lambda h: h * HD, 0, HD)]).astype(BF)
    w["wuk"] = w_uk[l].astype(BF)
    w["wuv"] = w_uv[l].astype(BF)
    w["gcq"] = g_cq[l][None, :]
    w["gckv"] = g_ckv[l][None, :]
    w["gq"] = _slot_vec(g_qn[l], g_qr[l])
    w["gk"] = _slot_vec(g_kn[l], jnp.zeros((ROPE_DIM,), F32))
    w["gk_q"] = _slot_vec(g_kn[l], jnp.ones((ROPE_DIM,), F32))
    w["gkr"] = jnp.pad(g_kr[l], (0, LANES - ROPE_DIM))[None, :]
    w["gfq"] = jnp.tile(g_fq[l], H_F)[None, :] * (HD ** -0.5)
    w["gfk"] = jnp.tile(g_fk[l], H_F)[None, :]
    w["bf"] = jnp.pad(b_f[l], (0, LANES - H_F))[None, :]
    w["wc"] = jnp.pad(w_conv[l], ((0, 8 - CONV_W), (0, 0)))
    w["gret"] = g_ret[l][None, :]
    offs = np.cumsum((0, H_A * HD, H_F * HD, W_C, H_R * DV_R))
    w["wbr"] = tuple(w_br[l][offs[m]:offs[m + 1]].astype(BF) for m in range(4))
    w["wo"] = w_out[l].astype(BF)
    return w


def kernel(x_prompt, x_sample, cache_mla_ckv, cache_mla_krope, cache_fox_k, cache_fox_v, cache_fox_logf,
           state_conv, state_ret, page_table, meta, g_norm, w_in, b_f, g_cq, g_ckv, w_uq, w_uk, w_uv,
           g_mla_qn, g_mla_qr, g_mla_kn, g_mla_kr, g_fox_q, g_fox_k, w_conv, g_ret, w_br, w_out):
    nb, seq, _ = x_prompt.shape
    db, n_new, _ = x_sample.shape
    depth = w_in.shape[0]
    n_pages = page_table.shape[1]
    past = n_pages * PAGE
    pf = (-N_META) % QB
    tp = pf + N_META + seq
    scale_a = (HD + ROPE_DIM) ** -0.5

    mq = _group_mean_matrix()
    m64 = _block_mean_matrix(H_F * HD, HD)
    place = np.zeros((LANES, H_A * SLOT), np.float32)
    for h in range(H_A):
        place[np.arange(ROPE_DIM), h * SLOT + HD + np.arange(ROPE_DIM)] = 1.0
    place = jnp.asarray(place, BF)
    ts = _pick(tp, (640, 384, 128))
    tri = jnp.asarray(np.tril(np.ones((ts, ts), np.float32)), BF)
    ind = jnp.asarray(np.kron(np.eye(H_A, dtype=np.float32), np.full((1, HD), 1.0 / HD, np.float32)), BF)
    upper = jnp.asarray(np.concatenate([np.triu(np.ones((PAGE, PAGE), np.float32), 1).T,
                                        np.ones((PAGE, PAGE), np.float32)], axis=1), BF)
    rows = 8 * n_new
    bd = np.zeros((rows, H_A * HD), np.float32)
    for r in range(rows):
        bd[r, (r % 8) * HD:(r % 8 + 1) * HD] = 1.0
    bd = jnp.asarray(bd)

    pos_p = jnp.tile(jnp.arange(tp) - pf, nb)
    pos_s = jnp.tile(past + jnp.arange(n_new), db)
    tabs_p = _rope_tables(pos_p, ROPE_DIM, HD, SLOT, True) + _rope_tables(pos_p, ROPE_DIM, 0, LANES, False)
    tabs_s = _rope_tables(pos_s, ROPE_DIM, HD, SLOT, True) + _rope_tables(pos_s, ROPE_DIM, 0, LANES, False)
    rc_p, rs_p = _rope_tables(jnp.arange(tp) - pf, DK_R, 0, DK_R, False)
    rc_p, rs_p = jnp.tile(rc_p, (1, 2)), jnp.tile(rs_p, (1, 2))
    pos_s8 = past + jnp.arange(8)
    rc_s, rs_s = _rope_tables(pos_s8, DK_R, 0, DK_R, False)
    rc_s, rs_s = jnp.tile(rc_s, (1, 2)), jnp.tile(rs_s, (1, 2))
    ret_p = (rc_p, rs_p) + _ret_tables(QB)
    ret_s = (rc_s, rs_s) + _ret_tables(n_new)

    krt_pool = jnp.swapaxes(cache_mla_krope, 2, 3)
    kt_pool = jnp.transpose(cache_fox_k, (0, 1, 3, 4, 2))
    vt_pool = jnp.transpose(cache_fox_v, (0, 1, 3, 4, 2))
    lft_pool = jnp.swapaxes(cache_fox_logf, 2, 3)
    pools = (cache_mla_ckv, krt_pool, kt_pool, vt_pool, lft_pool)

    h_p = jnp.concatenate([jnp.zeros((nb, pf, D_MODEL), F32),
                           jnp.broadcast_to(meta[None], (nb, N_META, D_MODEL)), x_prompt], axis=1)
    h_p = h_p.reshape(nb * tp, D_MODEL)
    h_s = x_sample.reshape(db * n_new, D_MODEL)
    zeros_state = jnp.zeros((nb, H_R * DK_R, DV_R), F32)
    qmul_p = jnp.full((1, H_A * SLOT), scale_a * LOG2E, F32)

    new_p = [[] for _ in range(7)]
    new_s = [[] for _ in range(7)]
    for l in range(depth):
        w = _prep_weights(l, g_norm, w_in, b_f, g_cq, g_ckv, w_uq, w_uk, w_uv, g_mla_qn, g_mla_qr,
                          g_mla_kn, g_mla_kr, g_fox_q, g_fox_k, w_conv, g_ret, w_br, w_out)
        mla_w = (w["gcq"], w["wuq"], mq, w["gq"], qmul_p, w["gckv"], w["gkr"], w["wuk_slot"], w["gk"], place,
                 w["wuv"])
        fox_w = (w["gfq"] * LOG2E, w["gfk"], m64, w["bf"])
        fox_ws = (w["gfq"],) + fox_w[1:]

        p = _proj(h_p, w["g_norm"], w["w_in"])
        q_a, ckvn, krr, k_a, v_a = _mla_prep(p, tabs_p, mla_w, True)
        fq, fkn, fkb, fvb, logf = _fox_prep(p, fox_w)
        fcum, o_c, conv_st = _seq_prep(p, logf, tri, w["wc"], nb, tp, pf)
        bias = fcum[:, :H_F].reshape(nb, tp, N_PAIR, 2).transpose(0, 2, 3, 1)
        o_a = _flash(q_a, k_a, v_a, None, nb, tp, pf)
        o_b = _flash(fq, fkb, fvb, bias, nb, tp, pf)
        o_d, s_fin = _retention(p, ret_p, w["gret"], zeros_state, nb, tp // QB, QB, pf, None)
        h_p = _merge(h_p, (o_a, o_b, o_c, o_d), p, w["wbr"], w["wo"])
        off_v, w_v = _SEG["f_v"]
        st = (ckvn.reshape(nb, tp, KV_RANK)[:, pf:],
              krr.reshape(nb, tp, LANES)[:, pf:, :ROPE_DIM],
              fkn.reshape(nb, tp, H_F, HD)[:, pf:],
              p[:, off_v:off_v + w_v].reshape(nb, tp, H_F, HD)[:, pf:],
              logf.reshape(nb, tp, LANES)[:, pf:, :H_F],
              conv_st,
              s_fin.reshape(nb, H_R, DK_R, DV_R))
        for j in range(7):
            new_p[j].append(st[j])

        ps = _proj(h_s, w["g_norm"], w["w_in"])
        mla_ws = mla_w[:4] + (scale_a * w["gk_q"],) + mla_w[5:]
        qs, ckvn_s, krr_s = _mla_prep(ps, tabs_s, mla_ws, False)
        fq_s, fkn_s, _, _, logf_s = _fox_prep(ps, fox_ws)
        pad8 = lambda a: jnp.pad(a.reshape(db, n_new, a.shape[-1]), ((0, 0), (0, 8 - n_new), (0, 0)))
        ps3 = pad8(ps)
        o_c_s, conv_s, fnew = _sample_local(ps3, pad8(logf_s), state_conv[l], w["wc"], n_new)
        o_d_s, ret_s_new = _retention(ps3, ret_s, w["gret"], state_ret[l].reshape(db, H_R * DK_R, DV_R),
                                      db, 1, 8, 0, None)
        q4 = qs.reshape(db, n_new, H_A, SLOT)
        eye = jnp.eye(H_A, dtype=BF)
        qn_bd = (q4[:, :, :, None, :HD] * eye[None, None, :, :, None]).reshape(db, rows, H_A * HD)
        qr_rep = q4[:, :, :, HD:HD + ROPE_DIM].reshape(db, rows, ROPE_DIM)
        f4 = fq_s.reshape(db, n_new, H_F, HD)
        qf_bd = (f4[:, :, :, None, :] * eye[None, None, :, :, None]).reshape(db, rows, H_F * HD)
        fn = fnew[:, :n_new, :H_F]
        fcol = jnp.broadcast_to(fn.reshape(db, rows, 1), (db, rows, LANES))
        frow = jnp.broadcast_to(jnp.pad(fn.transpose(0, 2, 1), ((0, 0), (0, 0), (0, LANES - n_new)))[:, None],
                                (db, n_new, H_F, LANES)).reshape(db, rows, LANES)
        off_v, w_v = _SEG["f_v"]
        fv_s = ps[:, off_v:off_v + w_v]
        consts = (w["wuk"], w["wuv"], ind, upper, bd)
        o_a_s, o_b_s = _sample_attn(l, page_table, qn_bd, qr_rep, pad8(ckvn_s), pad8(krr_s[:, :ROPE_DIM]),
                                    qf_bd, pad8(fkn_s), pad8(fv_s), fcol, frow, consts, pools, n_new)
        unpad = lambda a: a[:, :n_new].reshape(db * n_new, a.shape[-1])
        h_s = _merge(h_s, (o_a_s.reshape(db * n_new, -1), o_b_s.reshape(db * n_new, -1), unpad(o_c_s), unpad(o_d_s)),
                     ps, w["wbr"], w["wo"])
        st = (ckvn_s.reshape(db, n_new, KV_RANK),
              krr_s[:, :ROPE_DIM].reshape(db, n_new, ROPE_DIM),
              fkn_s.reshape(db, n_new, H_F, HD),
              fv_s.reshape(db, n_new, H_F, HD),
              logf_s[:, :H_F].reshape(db, n_new, H_F),
              conv_s,
              ret_s_new.reshape(db, H_R, DK_R, DV_R))
        for j in range(7):
            new_s[j].append(st[j])

    outs_p = [jnp.stack(a) for a in new_p]
    outs_s = [jnp.stack(a) for a in new_s]
    y_prompt = h_p.reshape(nb, tp, D_MODEL)[:, pf + N_META:]
    y_sample = h_s.reshape(db, n_new, D_MODEL)
    return (y_prompt, y_sample, *outs_p, *outs_s)
```
